```python
import math
import jax, jax.numpy as jnp
from jax import lax
import numpy as np

D_MODEL = 2048
BATCH = 1
SEQ = 16384
DEPTH = 1
DEC_BATCH = 8
DEC_SEQ = 32
PAST_LEN = 1024

CHUNK = 64
Q_BLOCK = 128
N_DIFF_HEADS = 8
DIFF_HEAD_DIM = 64
DIFF_V_DIM = 2 * DIFF_HEAD_DIM
ROT_DIM = DIFF_HEAD_DIM // 4
ROPE_THETA = 500000.0
ATT_SCALE = DIFF_HEAD_DIM ** -0.5
N_GMLP_GROUPS = 8
GMLP_GROUP_DIM = 128
GMLP_CHUNK = 128
GMLP_WIDTH = N_GMLP_GROUPS * GMLP_GROUP_DIM
ATT_QK_WIDTH = N_DIFF_HEADS * 2 * DIFF_HEAD_DIM
ATT_V_WIDTH = N_DIFF_HEADS * DIFF_V_DIM
MIX_WIDTH = GMLP_WIDTH + ATT_V_WIDTH
IN_WIDTH = 2 * GMLP_WIDTH + 2 * ATT_QK_WIDTH + ATT_V_WIDTH
D_FF = 4 * D_MODEL
CONV_W = 3
EPS = 1e-6

kernel_name = 'hybrid_gmlp_diffattn_convffn_stream_step'


def rms_norm(x, g):
    xf = x.astype(jnp.float32)
    y = xf * lax.rsqrt(jnp.mean(xf * xf, axis=-1, keepdims=True) + EPS)
    return (y * g.astype(jnp.float32)).astype(x.dtype)


def gelu(x):
    return jax.nn.gelu(x, approximate=True)


def rope_partial(x, pos):
    half = ROT_DIM // 2
    inv_freq = ROPE_THETA ** (-jnp.arange(half, dtype=jnp.float32) / half)
    ang = pos.astype(jnp.float32)[:, None] * inv_freq[None, :]
    cos = jnp.cos(ang)[None, :, None, None, :]
    sin = jnp.sin(ang)[None, :, None, None, :]
    xf = x.astype(jnp.float32)
    x1 = xf[..., :half]
    x2 = xf[..., half:ROT_DIM]
    out = jnp.concatenate([x1 * cos - x2 * sin, x2 * cos + x1 * sin, xf[..., ROT_DIM:]], axis=-1)
    return out.astype(x.dtype)


def split_projection(hn, w_in):
    b, s = hn.shape[:2]
    z = hn @ w_in
    o1 = 2 * GMLP_WIDTH
    o2 = o1 + ATT_QK_WIDTH
    o3 = o2 + ATT_QK_WIDTH
    u_raw = z[..., :GMLP_WIDTH]
    v_raw = z[..., GMLP_WIDTH:o1]
    q = z[..., o1:o2].reshape(b, s, N_DIFF_HEADS, 2, DIFF_HEAD_DIM)
    k = z[..., o2:o3].reshape(b, s, N_DIFF_HEADS, 2, DIFF_HEAD_DIM)
    v = z[..., o3:].reshape(b, s, N_DIFF_HEADS, DIFF_V_DIM)
    return u_raw, v_raw, q, k, v


def spatial_gate(u, vn, w_s, b_s, length):
    b, s = vn.shape[:2]
    vc = vn.reshape(b, s // length, length, N_GMLP_GROUPS, GMLP_GROUP_DIM)
    w = jnp.tril(w_s[:, :length, :length])
    bias = b_s[:, :length].T[None, None, :, :, None]
    sg = jnp.einsum('gij,bnjgc->bnigc', w, vc) + bias
    return u * sg.reshape(b, s, GMLP_WIDTH)


def diff_lambda(lq1, lk1, lq2, lk2, lam_init):
    d1 = jnp.sum(lq1.astype(jnp.float32) * lk1.astype(jnp.float32))
    d2 = jnp.sum(lq2.astype(jnp.float32) * lk2.astype(jnp.float32))
    return jnp.exp(d1) - jnp.exp(d2) + lam_init


def diff_scores_to_out(sc, allowed, lam, v):
    p = jax.nn.softmax(jnp.where(allowed, sc, -jnp.inf), axis=-1)
    a = p[:, :, 0] - lam * p[:, :, 1]
    return jnp.einsum('bhqk,bkhe->bqhe', a, v.astype(jnp.float32))


def prompt_diff_attention(q, k, v, lam):
    b, s = q.shape[:2]
    k_chunk = jnp.arange(s) // CHUNK

    def block(start):
        qb = lax.dynamic_slice_in_dim(q, start, Q_BLOCK, axis=1)
        sc = jnp.einsum('bqhcd,bkhcd->bhcqk', qb, k, preferred_element_type=jnp.float32) * ATT_SCALE
        q_chunk = (start + jnp.arange(Q_BLOCK)) // CHUNK
        return diff_scores_to_out(sc, k_chunk[None, :] <= q_chunk[:, None], lam, v)

    o = lax.map(block, jnp.arange(0, s, Q_BLOCK))
    return o.transpose(1, 0, 2, 3, 4).reshape(b, s, N_DIFF_HEADS, DIFF_V_DIM)


def sample_diff_attention(q, k_all, v_all, q_pos, lam):
    sc = jnp.einsum('bqhcd,bkhcd->bhcqk', q, k_all, preferred_element_type=jnp.float32) * ATT_SCALE
    k_pos = jnp.arange(k_all.shape[1])
    allowed = (k_pos // CHUNK)[None, :] <= (q_pos // CHUNK)[:, None]
    return diff_scores_to_out(sc, allowed, lam, v_all)


def causal_dwconv(h, hist, w, bias):
    t = h.shape[1]
    hp = jnp.concatenate([hist.astype(h.dtype), h], axis=1)
    y = sum(hp[:, j:j + t] * w[j] for j in range(CONV_W)) + bias
    return y, hp[:, t:]


def hybrid_layer(x, pos, attend, gmlp_len, conv_hist, lam_init,
                 g_mix_pre, w_in, gmlp_v_gain, gmlp_w_s, gmlp_b_s, subln_g, w_out, g_mix_post,
                 g_ffn_pre, w_up, conv_w, conv_b, w_down, g_ffn_post):
    b, s = x.shape[:2]
    hn = rms_norm(x, g_mix_pre)
    u_raw, v_raw, q, k, v = split_projection(hn, w_in)
    u = gelu(u_raw)
    vn = rms_norm(gelu(v_raw), gmlp_v_gain).reshape(b, s, N_GMLP_GROUPS, GMLP_GROUP_DIM)
    out_a = spatial_gate(u, vn, gmlp_w_s, gmlp_b_s, gmlp_len)
    q = rope_partial(q, pos)
    k = rope_partial(k, pos)
    o = attend(q, k, v)
    out_b = (rms_norm(o, subln_g) * (1.0 - lam_init)).reshape(b, s, ATT_V_WIDTH).astype(x.dtype)
    mix = jnp.concatenate([out_a, out_b], axis=-1) @ w_out
    x = x + rms_norm(mix, g_mix_post)
    up = rms_norm(x, g_ffn_pre) @ w_up
    up_c, new_hist = causal_dwconv(up, conv_hist, conv_w, conv_b)
    gate, val = jnp.split(up_c, 2, axis=-1)
    f = (gelu(gate) * val) @ w_down
    x = x + rms_norm(f, g_ffn_post)
    return x, k, v, vn, new_hist


def setup_inputs(seed: int = 0) -> dict:
    key = jax.random.key(seed)
    ks = jax.random.split(key, 24)

    def nrm(k, shape, scale):
        return jax.random.normal(k, shape, jnp.float32) * scale

    def gain(k, n):
        return 1.0 + nrm(k, (DEPTH, n), 0.05)

    return {
        'x_prompt': nrm(ks[0], (BATCH, SEQ, D_MODEL), 1.0),
        'x_sample': nrm(ks[1], (DEC_BATCH, DEC_SEQ, D_MODEL), 1.0),
        'cache_k': nrm(ks[2], (DEPTH, DEC_BATCH, PAST_LEN, N_DIFF_HEADS, 2, DIFF_HEAD_DIM), 1.0),
        'cache_v': nrm(ks[3], (DEPTH, DEC_BATCH, PAST_LEN, N_DIFF_HEADS, DIFF_V_DIM), 1.0),
        'state_conv': nrm(ks[4], (DEPTH, DEC_BATCH, CONV_W - 1, 2 * D_FF), 1.0),
        'g_mix_pre': gain(ks[5], D_MODEL),
        'w_in': nrm(ks[6], (DEPTH, D_MODEL, IN_WIDTH), D_MODEL ** -0.5),
        'gmlp_v_gain': gain(ks[7], GMLP_WIDTH),
        'gmlp_w_s': nrm(ks[8], (DEPTH, N_GMLP_GROUPS, GMLP_CHUNK, GMLP_CHUNK), GMLP_CHUNK ** -0.5),
        'gmlp_b_s': 1.0 + nrm(ks[9], (DEPTH, N_GMLP_GROUPS, GMLP_CHUNK), 0.1),
        'lambda_q1': nrm(ks[10], (DEPTH, DIFF_HEAD_DIM), 0.1),
        'lambda_k1': nrm(ks[11], (DEPTH, DIFF_HEAD_DIM), 0.1),
        'lambda_q2': nrm(ks[12], (DEPTH, DIFF_HEAD_DIM), 0.1),
        'lambda_k2': nrm(ks[13], (DEPTH, DIFF_HEAD_DIM), 0.1),
        'subln_g': gain(ks[14], DIFF_V_DIM),
        'w_out': nrm(ks[15], (DEPTH, MIX_WIDTH, D_MODEL), MIX_WIDTH ** -0.5),
        'g_mix_post': gain(ks[16], D_MODEL),
        'g_ffn_pre': gain(ks[17], D_MODEL),
        'w_up': nrm(ks[18], (DEPTH, D_MODEL, 2 * D_FF), D_MODEL ** -0.5),
        'conv_w': nrm(ks[19], (DEPTH, CONV_W, 2 * D_FF), CONV_W ** -0.5),
        'conv_b': nrm(ks[20], (DEPTH, 2 * D_FF), 0.02),
        'w_down': nrm(ks[21], (DEPTH, D_FF, D_MODEL), D_FF ** -0.5),
        'g_ffn_post': gain(ks[22], D_MODEL),
    }


def reference(x_prompt, x_sample, cache_k, cache_v, state_conv,
              g_mix_pre, w_in, gmlp_v_gain, gmlp_w_s, gmlp_b_s,
              lambda_q1, lambda_k1, lambda_q2, lambda_k2, subln_g, w_out, g_mix_post,
              g_ffn_pre, w_up, conv_w, conv_b, w_down, g_ffn_post):
    b_p, s_p = x_prompt.shape[:2]
    t_new = x_sample.shape[1]
    past_len = cache_k.shape[2]
    pos_p = jnp.arange(s_p)
    pos_s = past_len + jnp.arange(t_new)
    zero_hist = jnp.zeros((b_p, CONV_W - 1, 2 * D_FF), x_prompt.dtype)

    yp, ys = x_prompt, x_sample
    kp_l, vp_l, cp_l, ks_l, vs_l, gs_l, cs_l = [], [], [], [], [], [], []
    for l in range(DEPTH):
        lam_init = 0.8 - 0.6 * math.exp(-0.3 * l)
        lam = diff_lambda(lambda_q1[l], lambda_k1[l], lambda_q2[l], lambda_k2[l], lam_init)
        lw = (g_mix_pre[l], w_in[l], gmlp_v_gain[l], gmlp_w_s[l], gmlp_b_s[l], subln_g[l],
              w_out[l], g_mix_post[l], g_ffn_pre[l], w_up[l], conv_w[l], conv_b[l],
              w_down[l], g_ffn_post[l])

        attend_p = lambda q, k, v: prompt_diff_attention(q, k, v, lam)
        yp, kp, vp, _, cp = hybrid_layer(yp, pos_p, attend_p, GMLP_CHUNK, zero_hist, lam_init, *lw)

        ck, cv = cache_k[l], cache_v[l]
        attend_s = lambda q, k, v: sample_diff_attention(
            q, jnp.concatenate([ck.astype(k.dtype), k], axis=1),
            jnp.concatenate([cv.astype(v.dtype), v], axis=1), pos_s, lam)
        ys, ks, vs, gs, cs = hybrid_layer(ys, pos_s, attend_s, t_new, state_conv[l], lam_init, *lw)

        kp_l.append(kp); vp_l.append(vp); cp_l.append(cp)
        ks_l.append(ks); vs_l.append(vs); gs_l.append(gs); cs_l.append(cs)

    new_k_prompt = jnp.stack(kp_l)
    new_v_prompt = jnp.stack(vp_l)
    new_conv_prompt = jnp.stack(cp_l)
    new_k_sample = jnp.stack(ks_l)
    new_v_sample = jnp.stack(vs_l)
    new_gmlp_v_sample = jnp.stack(gs_l)
    new_conv_sample = jnp.stack(cs_l)
    return (yp, ys, new_k_prompt, new_v_prompt, new_conv_prompt,
            new_k_sample, new_v_sample, new_gmlp_v_sample, new_conv_sample)
```

```python
import functools
import math

import jax
import jax.numpy as jnp
from jax import lax
from jax.experimental import pallas as pl
from jax.experimental.pallas import tpu as pltpu

F32 = jnp.float32
BF16 = jnp.bfloat16

N_HEADS = 8
HEAD_DIM = 64
HEAD_W = 2 * HEAD_DIM
CHUNK = 64
ROT_DIM = HEAD_DIM // 4
ROT_HALF = ROT_DIM // 2
ROPE_THETA = 500000.0
ATT_SCALE = HEAD_DIM ** -0.5
N_GROUPS = 8
GROUP_DIM = 128
GMLP_W = N_GROUPS * GROUP_DIM
ATT_W = N_HEADS * HEAD_W
CONV_W = 3
EPS = 1e-6
LOG2E = 1.4426950408889634

LANES = 128
SUBLANES = 8
VMEM_BYTES_V7X = 64 * 1024 * 1024

ROW_TILE = 512
FF_TILE = 512
ATT_Q_TILE = 512
ATT_K_TILE = 256
N_PROJ_BLOCKS = 5


def _vmem_limit(estimate_bytes):
    return int(min(VMEM_BYTES_V7X - (4 << 20), max(32 << 20, estimate_bytes)))


def _gelu(x):
    return 0.5 * x * (1.0 + jnp.tanh(math.sqrt(2.0 / math.pi) * (x + 0.044715 * (x * x * x))))


def _rms(x, g):
    return x * lax.rsqrt(jnp.mean(x * x, axis=-1, keepdims=True) + EPS) * g


def _rope(z, cos, sin_lo, sin_hi):
    outs = []
    for s in range(z.shape[1] // LANES):
        xs = z[:, s * LANES:(s + 1) * LANES]
        nxt = pltpu.roll(xs, LANES - ROT_HALF, axis=1)
        prv = pltpu.roll(xs, ROT_HALF, axis=1)
        outs.append(xs * cos + nxt * sin_lo + prv * sin_hi)
    return jnp.concatenate(outs, axis=1)


def _inproj_kernel(x_ref, gpre_ref, w_ref, gv_ref, ws_ref, bs_ref, cos_ref, slo_ref, shi_ref,
                   *refs, gmlp_len, transposed, emit_vn, k_tile):
    if emit_vn:
        outa_ref, q_ref, k32_ref, kbf_ref, v32_ref, vbf_ref, vn_ref, hn_s, u_s = refs
    else:
        outa_ref, q_ref, k32_ref, kbf_ref, v32_ref, vbf_ref, hn_s, u_s = refs
        vn_ref = None
    j = pl.program_id(1)
    tm = x_ref.shape[0]

    @pl.when(j == 0)
    def _():
        hn_s[...] = _rms(x_ref[...], gpre_ref[...]).astype(BF16)

    z = jnp.dot(hn_s[...], w_ref[...], preferred_element_type=F32)

    @pl.when(j == 0)
    def _():
        u_s[...] = _gelu(z)

    @pl.when(j == 1)
    def _():
        vn = _rms(_gelu(z), gv_ref[...])
        if emit_vn:
            vn_ref[...] = vn
        vnb = vn.astype(BF16)
        ln = gmlp_len
        r = lax.broadcasted_iota(jnp.int32, (ln, ln), 0)
        c = lax.broadcasted_iota(jnp.int32, (ln, ln), 1)
        causal = c <= r
        for g in range(N_GROUPS):
            wg = jnp.where(causal, ws_ref[g], 0.0).astype(BF16)
            cols = slice(g * GROUP_DIM, (g + 1) * GROUP_DIM)
            bias = bs_ref[:, cols]
            for ch in range(tm // ln):
                rows = slice(ch * ln, (ch + 1) * ln)
                sg = jnp.dot(wg, vnb[rows, cols], preferred_element_type=F32) + bias
                outa_ref[rows, cols] = (u_s[rows, cols] * sg).astype(BF16)

    @pl.when(j == 2)
    def _():
        q = _rope(z, cos_ref[...], slo_ref[...], shi_ref[...]) * (ATT_SCALE * LOG2E)
        if transposed:
            q_ref[...] = q.T.astype(BF16)
        else:
            q_ref[...] = q.astype(BF16)

    @pl.when(j == 3)
    def _():
        k = _rope(z, cos_ref[...], slo_ref[...], shi_ref[...])
        k32_ref[...] = k
        kbf_ref[...] = k.astype(BF16)

    @pl.when(j == 4)
    def _():
        v32_ref[...] = z
        if transposed:
            vt = z.T.astype(BF16)
            for cb in range(tm // k_tile):
                vbf_ref[cb] = vt[:, cb * k_tile:(cb + 1) * k_tile]
        else:
            vbf_ref[...] = z.astype(BF16)


def _rope_tables(pos):
    n = pos.shape[0]
    inv_freq = ROPE_THETA ** (-jnp.arange(ROT_HALF, dtype=F32) / ROT_HALF)
    ang = pos.astype(F32)[:, None] * inv_freq[None, :]
    cos, sin = jnp.cos(ang), jnp.sin(ang)
    rest = HEAD_DIM - ROT_DIM
    cos64 = jnp.concatenate([cos, cos, jnp.ones((n, rest), F32)], axis=1)
    lo64 = jnp.concatenate([-sin, jnp.zeros((n, HEAD_DIM - ROT_HALF), F32)], axis=1)
    hi64 = jnp.concatenate([jnp.zeros((n, ROT_HALF), F32), sin, jnp.zeros((n, rest), F32)], axis=1)
    two = lambda t: jnp.concatenate([t, t], axis=1)
    return two(cos64), two(lo64), two(hi64)


def _inproj(x, pos, g_pre, w_in, gv, w_s, b_s, *, gmlp_len, transposed, emit_vn, tm):
    rows, d = x.shape
    assert rows % tm == 0 and tm % gmlp_len == 0
    k_tile = min(ATT_K_TILE, tm)
    cos, slo, shi = _rope_tables(pos)
    ws = w_s[:, :gmlp_len, :gmlp_len]
    bs = jnp.repeat(b_s[:, :gmlp_len].T, GROUP_DIM, axis=1)

    row_blk = lambda w, dt=None: pl.BlockSpec((tm, w), lambda i, j: (i, 0))
    const = lambda shape: pl.BlockSpec(shape, lambda i, j: (0,) * len(shape))
    in_specs = [
        row_blk(d), const((1, d)),
        pl.BlockSpec((d, GMLP_W), lambda i, j: (0, j)),
        const((1, GMLP_W)), const(ws.shape), const(bs.shape),
        row_blk(LANES), row_blk(LANES), row_blk(LANES),
    ]
    if transposed:
        q_shape, q_spec = (ATT_W, rows), pl.BlockSpec((ATT_W, tm), lambda i, j: (0, i))
        v_shape = (rows // k_tile, ATT_W, k_tile)
        v_spec = pl.BlockSpec((tm // k_tile, ATT_W, k_tile), lambda i, j: (i, 0, 0))
    else:
        q_shape, q_spec = (rows, ATT_W), row_blk(ATT_W)
        v_shape, v_spec = (rows, ATT_W), row_blk(ATT_W)
    out_shape = [
        jax.ShapeDtypeStruct((rows, GMLP_W), BF16), jax.ShapeDtypeStruct(q_shape, BF16),
        jax.ShapeDtypeStruct((rows, ATT_W), F32), jax.ShapeDtypeStruct((rows, ATT_W), BF16),
        jax.ShapeDtypeStruct((rows, ATT_W), F32), jax.ShapeDtypeStruct(v_shape, BF16),
    ]
    out_specs = [row_blk(GMLP_W), q_spec, row_blk(ATT_W), row_blk(ATT_W), row_blk(ATT_W), v_spec]
    if emit_vn:
        out_shape.append(jax.ShapeDtypeStruct((rows, GMLP_W), F32))
        out_specs.append(row_blk(GMLP_W))
    est = (2 * tm * d * 4 + 2 * d * GMLP_W * 2 + tm * d * 2 + tm * GMLP_W * 4
           + 2 * tm * GMLP_W * (2 + 2 + 4 + 2 + 4 + 2 + 4) + 8 * tm * GMLP_W * 4)
    return pl.pallas_call(
        functools.partial(_inproj_kernel, gmlp_len=gmlp_len, transposed=transposed,
                          emit_vn=emit_vn, k_tile=k_tile),
        grid=(rows // tm, N_PROJ_BLOCKS),
        in_specs=in_specs, out_specs=out_specs, out_shape=out_shape,
        scratch_shapes=[pltpu.VMEM((tm, d), BF16), pltpu.VMEM((tm, GMLP_W), F32)],
        compiler_params=pltpu.CompilerParams(
            dimension_semantics=("arbitrary", "arbitrary"), vmem_limit_bytes=_vmem_limit(est)),
        name="inproj",
    )(x, g_pre, w_in, gv, ws, bs, cos, slo, shi)


def _lambda(lq1_ref, lk1_ref, lq2_ref, lk2_ref, lam_init):
    d1 = jnp.sum(lq1_ref[...] * lk1_ref[...], axis=-1, keepdims=True)
    d2 = jnp.sum(lq2_ref[...] * lk2_ref[...], axis=-1, keepdims=True)
    return jnp.exp(d1) - jnp.exp(d2) + lam_init


def _prompt_attn_kernel(qt_ref, k_ref, vt_ref, lq1_ref, lk1_ref, lq2_ref, lk2_ref, g_ref, out_ref,
                        m_s, l_s, acc_s, *, tq, tk, lam_init):
    qi = pl.program_id(1)
    qt = qt_ref[...]
    comp = lax.broadcasted_iota(jnp.int32, qt.shape, 0) < HEAD_DIM
    zero = jnp.zeros_like(qt)
    qs = (jnp.where(comp, qt, zero), jnp.where(comp, zero, qt))

    m_s[...] = jnp.full(m_s.shape, -jnp.inf, F32)
    l_s[...] = jnp.zeros(l_s.shape, F32)
    acc_s[...] = jnp.zeros(acc_s.shape, F32)

    def step(kb, key_offset):
        kblk = k_ref[pl.ds(pl.multiple_of(kb * tk, tk), tk), :]
        vtb = vt_ref[kb]
        if key_offset is not None:
            kc = (lax.broadcasted_iota(jnp.int32, (tk, tq), 0) + key_offset) // CHUNK
            qc = lax.broadcasted_iota(jnp.int32, (tk, tq), 1) // CHUNK
            allowed = kc <= qc
        for c in range(2):
            s = jnp.dot(kblk, qs[c], preferred_element_type=F32)
            if key_offset is not None:
                s = jnp.where(allowed, s, -jnp.inf)
            m_old = m_s[c]
            m_new = jnp.maximum(m_old, jnp.max(s, axis=0, keepdims=True))
            alpha = jnp.exp2(m_old - m_new)
            p = jnp.exp2(s - m_new)
            l_s[c] = alpha * l_s[c] + jnp.sum(p, axis=0, keepdims=True)
            acc_s[c] = alpha * acc_s[c] + jnp.dot(vtb, p.astype(BF16), preferred_element_type=F32)
            m_s[c] = m_new

    n_full = qi * (tq // tk)

    def body(kb, carry):
        step(kb, None)
        return carry

    lax.fori_loop(0, n_full, body, 0)
    for d in range(tq // tk):
        step(n_full + d, d * tk)

    lam = _lambda(lq1_ref, lk1_ref, lq2_ref, lk2_ref, lam_init)
    o = acc_s[0] / l_s[0] - lam * (acc_s[1] / l_s[1])
    on = o * lax.rsqrt(jnp.mean(o * o, axis=0, keepdims=True) + EPS)
    out_ref[...] = (on.T * (g_ref[...] * (1.0 - lam_init))).astype(BF16)


def _prompt_attention(qt, kbf, vt, lam_vecs, subln_g, *, lam_init, tq, tk):
    rows = kbf.shape[0]
    assert rows % tq == 0 and tq % tk == 0 and tk % CHUNK == 0 and vt.shape[2] == tk
    vec = pl.BlockSpec((1, HEAD_DIM), lambda h, i: (0, 0))
    est = 2 * 2 * rows * HEAD_W * 2 + 2 * HEAD_W * tq * 4 + 12 * tk * tq * 4
    return pl.pallas_call(
        functools.partial(_prompt_attn_kernel, tq=tq, tk=tk, lam_init=lam_init),
        grid=(N_HEADS, rows // tq),
        in_specs=[
            pl.BlockSpec((HEAD_W, tq), lambda h, i: (h, i)),
            pl.BlockSpec((rows, HEAD_W), lambda h, i: (0, h)),
            pl.BlockSpec((rows // tk, HEAD_W, tk), lambda h, i: (0, h, 0)),
            vec, vec, vec, vec,
            pl.BlockSpec((1, HEAD_W), lambda h, i: (0, 0)),
        ],
        out_specs=pl.BlockSpec((tq, HEAD_W), lambda h, i: (i, h)),
        out_shape=jax.ShapeDtypeStruct((rows, ATT_W), BF16),
        scratch_shapes=[pltpu.VMEM((2, 1, tq), F32), pltpu.VMEM((2, 1, tq), F32),
                        pltpu.VMEM((2, HEAD_W, tq), F32)],
        compiler_params=pltpu.CompilerParams(
            dimension_semantics=("arbitrary", "arbitrary"), vmem_limit_bytes=_vmem_limit(est)),
        name="prompt_attn",
    )(qt, kbf, vt, *lam_vecs, subln_g)


def _sample_attn_kernel(q_ref, kc_ref, vc_ref, kn_ref, vn_ref, lq1_ref, lk1_ref, lq2_ref, lk2_ref,
                        g_ref, out_ref, *, past_len, lam_init):
    q = q_ref[...]
    t = q.shape[0]
    comp = lax.broadcasted_iota(jnp.int32, q.shape, 1) < HEAD_DIM
    zero = jnp.zeros_like(q)
    qs = (jnp.where(comp, q, zero), jnp.where(comp, zero, q))
    kc = kc_ref[0].astype(BF16)
    vc = vc_ref[0].astype(BF16)
    kn = kn_ref[...]
    vn = vn_ref[...]
    q_chunk = (past_len + lax.broadcasted_iota(jnp.int32, (t, 1), 0)) // CHUNK
    ok_c = lax.broadcasted_iota(jnp.int32, (t, past_len), 1) // CHUNK <= q_chunk
    ok_n = (past_len + lax.broadcasted_iota(jnp.int32, (t, t), 1)) // CHUNK <= q_chunk
    nt = (((1,), (1,)), ((), ()))
    outs = []
    for c in range(2):
        sc = lax.dot_general(qs[c], kc, nt, preferred_element_type=F32)
        sn = lax.dot_general(qs[c], kn, nt, preferred_element_type=F32)
        sc = jnp.where(ok_c, sc, -jnp.inf)
        sn = jnp.where(ok_n, sn, -jnp.inf)
        m = jnp.maximum(jnp.max(sc, axis=-1, keepdims=True), jnp.max(sn, axis=-1, keepdims=True))
        pc = jnp.exp2(sc - m)
        pn = jnp.exp2(sn - m)
        den = jnp.sum(pc, axis=-1, keepdims=True) + jnp.sum(pn, axis=-1, keepdims=True)
        num = (jnp.dot(pc.astype(BF16), vc, preferred_element_type=F32)
               + jnp.dot(pn.astype(BF16), vn, preferred_element_type=F32))
        outs.append(num / den)
    lam = _lambda(lq1_ref, lk1_ref, lq2_ref, lk2_ref, lam_init)
    o = outs[0] - lam * outs[1]
    out_ref[...] = (_rms(o, g_ref[...]) * (1.0 - lam_init)).astype(BF16)


def _sample_attention(q, kbf, vbf, cache_k, cache_v, lam_vecs, subln_g, *, lam_init, t_new):
    n_streams, past_len = cache_k.shape[:2]
    vec = pl.BlockSpec((1, HEAD_DIM), lambda b, h: (0, 0))
    new_blk = pl.BlockSpec((t_new, HEAD_W), lambda b, h: (b, h))
    cache_blk = pl.BlockSpec((1, past_len, HEAD_W), lambda b, h: (b, 0, h))
    return pl.pallas_call(
        functools.partial(_sample_attn_kernel, past_len=past_len, lam_init=lam_init),
        grid=(n_streams, N_HEADS),
        in_specs=[new_blk, cache_blk, cache_blk, new_blk, new_blk, vec, vec, vec, vec,
                  pl.BlockSpec((1, HEAD_W), lambda b, h: (0, 0))],
        out_specs=new_blk,
        out_shape=jax.ShapeDtypeStruct((n_streams * t_new, ATT_W), BF16),
        compiler_params=pltpu.CompilerParams(dimension_semantics=("arbitrary", "arbitrary")),
        name="sample_attn",
    )(q, cache_k, cache_v, kbf, vbf, *lam_vecs, subln_g)


def _outproj_kernel(x_ref, a_ref, b_ref, w_ref, gpost_ref, gffn_ref, x1_ref, hn_ref):
    mix = (jnp.dot(a_ref[...], w_ref[:GMLP_W, :], preferred_element_type=F32)
           + jnp.dot(b_ref[...], w_ref[GMLP_W:, :], preferred_element_type=F32))
    x1 = x_ref[...] + _rms(mix, gpost_ref[...])
    x1_ref[...] = x1
    hn_ref[...] = _rms(x1, gffn_ref[...]).astype(BF16)


def _outproj(x, out_a, out_b, w_out, g_post, g_ffn, *, tm):
    rows, d = x.shape
    row_blk = lambda w: pl.BlockSpec((tm, w), lambda i: (i, 0))
    const = lambda shape: pl.BlockSpec(shape, lambda i: (0,) * len(shape))
    est = 2 * tm * d * (4 + 4 + 2) + 2 * 2 * tm * GMLP_W * 2 + 2 * w_out.size * 2 + 4 * tm * d * 4
    return pl.pallas_call(
        _outproj_kernel,
        grid=(rows // tm,),
        in_specs=[row_blk(d), row_blk(GMLP_W), row_blk(ATT_W), const(w_out.shape),
                  const((1, d)), const((1, d))],
        out_specs=[row_blk(d), row_blk(d)],
        out_shape=[jax.ShapeDtypeStruct((rows, d), F32), jax.ShapeDtypeStruct((rows, d), BF16)],
        compiler_params=pltpu.CompilerParams(
            dimension_semantics=("arbitrary",), vmem_limit_bytes=_vmem_limit(est)),
        name="outproj",
    )(x, out_a, out_b, w_out, g_post, g_ffn)


def _ffn_kernel(hn_ref, x1_ref, wg_ref, wv_ref, cwg_ref, cwv_ref, cbg_ref, cbv_ref, hg_ref, hv_ref,
                wd_ref, gpost_ref, y_ref, cg_ref, cv_ref, acc_s, carg_s, carv_s,
                *, n_streams, stream_len):
    i = pl.program_id(0)
    j = pl.program_id(1)
    tm = hn_ref.shape[0]
    ft = wg_ref.shape[1]
    hn = hn_ref[...]
    ridx = lax.broadcasted_iota(jnp.int32, (SUBLANES, ft), 0)

    def conv_branch(w_ref, cw_ref, cb_ref, h_ref, car_s, cout_ref):
        up = jnp.dot(hn, w_ref[...], preferred_element_type=F32)
        if n_streams == 1:
            @pl.when(i == 0)
            def _():
                car_s[j] = jnp.concatenate(
                    [jnp.zeros((SUBLANES - (CONV_W - 1), ft), F32), h_ref[0]], axis=0)
            tail = car_s[j]
            hists = [(tail[SUBLANES - 2:SUBLANES - 1], tail[SUBLANES - 1:SUBLANES])]
        else:
            hists = [(h_ref[s, 0:1, :], h_ref[s, 1:2, :]) for s in range(n_streams)]
        r1 = pltpu.roll(up, 1, axis=0)
        r2 = pltpu.roll(up, 2, axis=0)
        p1, p2 = [], []
        for s, (h0, h1) in enumerate(hists):
            lo = s * stream_len
            p1.append(jnp.where(ridx == 0, h1, r1[lo:lo + SUBLANES]))
            p1.append(r1[lo + SUBLANES:lo + stream_len])
            p2.append(jnp.where(ridx == 0, h0, jnp.where(ridx == 1, h1, r2[lo:lo + SUBLANES])))
            p2.append(r2[lo + SUBLANES:lo + stream_len])
            cout_ref[0, s] = up[lo + stream_len - (CONV_W - 1):lo + stream_len]
        prev1 = jnp.concatenate(p1, axis=0)
        prev2 = jnp.concatenate(p2, axis=0)
        if n_streams == 1:
            car_s[j] = up[tm - SUBLANES:tm]
        cw = cw_ref[...]
        return prev2 * cw[0:1] + prev1 * cw[1:2] + up * cw[2:3] + cb_ref[...]

    gate = conv_branch(wg_ref, cwg_ref, cbg_ref, hg_ref, carg_s, cg_ref)
    val = conv_branch(wv_ref, cwv_ref, cbv_ref, hv_ref, carv_s, cv_ref)
    act = (_gelu(gate) * val).astype(BF16)
    part = jnp.dot(act, wd_ref[...], preferred_element_type=F32)

    @pl.when(j == 0)
    def _():
        acc_s[...] = part

    @pl.when(j > 0)
    def _():
        acc_s[...] += part

    @pl.when(j == pl.num_programs(1) - 1)
    def _():
        y_ref[...] = x1_ref[...] + _rms(acc_s[...], gpost_ref[...])


def _ffn(hn, x1, w_up, conv_w, conv_b, hist, w_down, g_post, *, tm, ft, n_streams):
    rows, d = x1.shape
    d_ff = w_down.shape[0]
    nj = d_ff // ft
    n_tiles = rows // tm
    stream_len = tm // n_streams
    assert rows % tm == 0 and d_ff % ft == 0 and (n_streams == 1 or n_tiles == 1)
    assert hist.shape == (n_streams, CONV_W - 1, 2 * d_ff) and stream_len % SUBLANES == 0
    gate_col = lambda i, j: (0, j)
    val_col = lambda i, j: (0, nj + j)
    row_blk = lambda w: pl.BlockSpec((tm, w), lambda i, j: (i, 0))
    hist_blk = lambda off: pl.BlockSpec((n_streams, CONV_W - 1, ft), lambda i, j: (0, 0, off + j))
    tail_blk = pl.BlockSpec((1, n_streams, CONV_W - 1, ft), lambda i, j: (i, 0, 0, j))
    tail_shape = jax.ShapeDtypeStruct((n_tiles, n_streams, CONV_W - 1, d_ff), F32)
    est = (2 * tm * d * (2 + 4 + 4) + tm * d * 4 + 2 * 3 * d * ft * 2 + 10 * tm * ft * 4
           + 2 * tm * d * 4)
    y, cg, cv = pl.pallas_call(
        functools.partial(_ffn_kernel, n_streams=n_streams, stream_len=stream_len),
        grid=(n_tiles, nj),
        in_specs=[
            row_blk(d), row_blk(d),
            pl.BlockSpec((d, ft), gate_col), pl.BlockSpec((d, ft), val_col),
            pl.BlockSpec((CONV_W, ft), gate_col), pl.BlockSpec((CONV_W, ft), val_col),
            pl.BlockSpec((1, ft), gate_col), pl.BlockSpec((1, ft), val_col),
            hist_blk(0), hist_blk(nj),
            pl.BlockSpec((ft, d), lambda i, j: (j, 0)),
            pl.BlockSpec((1, d), lambda i, j: (0, 0)),
        ],
        out_specs=[row_blk(d), tail_blk, tail_blk],
        out_shape=[jax.ShapeDtypeStruct((rows, d), F32), tail_shape, tail_shape],
        scratch_shapes=[pltpu.VMEM((tm, d), F32),
                        pltpu.VMEM((nj, SUBLANES, ft), F32), pltpu.VMEM((nj, SUBLANES, ft), F32)],
        compiler_params=pltpu.CompilerParams(
            dimension_semantics=("arbitrary", "arbitrary"), vmem_limit_bytes=_vmem_limit(est)),
        name="convffn",
    )(hn, x1, w_up, w_up, conv_w, conv_w, conv_b, conv_b, hist, hist, w_down, g_post)
    return y, jnp.concatenate([cg[-1], cv[-1]], axis=-1)


def _layer(x, pos, hist, attend, lw, *, gmlp_len, transposed, emit_vn, n_streams, tm):
    (g_mix_pre, w_in, gv, w_s, b_s, subln_g, w_out, g_mix_post, g_ffn_pre, w_up, conv_w, conv_b,
     w_down, g_ffn_post) = lw
    res = _inproj(x, pos, g_mix_pre, w_in, gv, w_s, b_s, gmlp_len=gmlp_len, transposed=transposed,
                  emit_vn=emit_vn, tm=tm)
    out_a, q, k32, kbf, v32, vbf = res[:6]
    vn = res[6] if emit_vn else None
    out_b = attend(q, kbf, vbf)
    x1, hn2 = _outproj(x, out_a, out_b, w_out, g_mix_post, g_ffn_pre, tm=tm)
    y, new_hist = _ffn(hn2, x1, w_up, conv_w, conv_b, hist, w_down, g_ffn_post,
                       tm=tm, ft=FF_TILE, n_streams=n_streams)
    return y, k32, v32, vn, new_hist


def kernel(x_prompt, x_sample, cache_k, cache_v, state_conv, g_mix_pre, w_in, gmlp_v_gain, gmlp_w_s,
           gmlp_b_s, lambda_q1, lambda_k1, lambda_q2, lambda_k2, subln_g, w_out, g_mix_post,
           g_ffn_pre, w_up, conv_w, conv_b, w_down, g_ffn_post):
    b_p, s_p, d = x_prompt.shape
    b_s, t_new, _ = x_sample.shape
    depth = w_in.shape[0]
    past_len = cache_k.shape[2]
    d_ff = w_down.shape[1]
    assert b_p == 1
    pos_p = jnp.arange(s_p)
    pos_s = jnp.tile(past_len + jnp.arange(t_new), b_s)
    zero_hist = jnp.zeros((b_p, CONV_W - 1, 2 * d_ff), F32)
    row = lambda a: a.reshape(1, -1)

    yp = x_prompt.reshape(b_p * s_p, d)
    ys = x_sample.reshape(b_s * t_new, d)
    outs = [[] for _ in range(7)]
    for l in range(depth):
        lam_init = 0.8 - 0.6 * math.exp(-0.3 * l)
        lam_vecs = tuple(row(a[l]) for a in (lambda_q1, lambda_k1, lambda_q2, lambda_k2))
        sub_g = row(subln_g[l])
        lw = (row(g_mix_pre[l]), w_in[l].astype(BF16), row(gmlp_v_gain[l]), gmlp_w_s[l], gmlp_b_s[l],
              sub_g, w_out[l].astype(BF16), row(g_mix_post[l]), row(g_ffn_pre[l]),
              w_up[l].astype(BF16), conv_w[l], row(conv_b[l]), w_down[l].astype(BF16),
              row(g_ffn_post[l]))

        tq = min(ATT_Q_TILE, s_p)
        attend_p = lambda q, k, v: _prompt_attention(
            q, k, v, lam_vecs, sub_g, lam_init=lam_init, tq=tq, tk=min(ATT_K_TILE, tq))
        yp, kp, vp, _, cp = _layer(yp, pos_p, zero_hist, attend_p, lw, gmlp_len=GROUP_DIM,
                                   transposed=True, emit_vn=False, n_streams=1,
                                   tm=min(ROW_TILE, s_p))

        ck = cache_k[l].reshape(b_s, past_len, ATT_W)
        cv = cache_v[l].reshape(b_s, past_len, ATT_W)
        attend_s = lambda q, k, v: _sample_attention(
            q, k, v, ck, cv, lam_vecs, sub_g, lam_init=lam_init, t_new=t_new)
        ys, ks, vs, gs, cs = _layer(ys, pos_s, state_conv[l], attend_s, lw, gmlp_len=t_new,
                                    transposed=False, emit_vn=True, n_streams=b_s,
                                    tm=b_s * t_new)

        for acc, val in zip(outs, (
                kp.reshape(b_p, s_p, N_HEADS, 2, HEAD_DIM), vp.reshape(b_p, s_p, N_HEADS, HEAD_W), cp,
                ks.reshape(b_s, t_new, N_HEADS, 2, HEAD_DIM), vs.reshape(b_s, t_new, N_HEADS, HEAD_W),
                gs.reshape(b_s, t_new, N_GROUPS, GROUP_DIM), cs)):
            acc.append(val)

    return (yp.reshape(b_p, s_p, d), ys.reshape(b_s, t_new, d)) + tuple(jnp.stack(o) for o in outs)
```

```python
import functools
import math

import jax
import jax.numpy as jnp
from jax import lax
from jax.experimental import pallas as pl
from jax.experimental.pallas import tpu as pltpu

F32 = jnp.float32
BF16 = jnp.bfloat16

N_HEADS = 8
HEAD_DIM = 64
HEAD_W = 2 * HEAD_DIM
CHUNK = 64
ROT_DIM = HEAD_DIM // 4
ROT_HALF = ROT_DIM // 2
ROPE_THETA = 500000.0
ATT_SCALE = HEAD_DIM ** -0.5
N_GROUPS = 8
GROUP_DIM = 128
GMLP_W = N_GROUPS * GROUP_DIM
ATT_W = N_HEADS * HEAD_W
CONV_W = 3
EPS = 1e-6
LOG2E = 1.4426950408889634

LANES = 128
SUBLANES = 8
VMEM_BYTES_V7X = 64 * 1024 * 1024

ROW_TILE = 512
FF_TILE = 512
FF_SUBTILE = 256
ATT_Q_TILE = 512
ATT_K_TILE = 256
N_PROJ_BLOCKS = 5


def _vmem_limit(estimate_bytes):
    return int(min(VMEM_BYTES_V7X - (4 << 20), max(32 << 20, estimate_bytes)))


def _gelu(x):
    a = -2.0 * math.sqrt(2.0 / math.pi) * LOG2E
    return x / (1.0 + jnp.exp2(x * (a + (a * 0.044715) * (x * x))))


def _rms(x, g):
    return x * lax.rsqrt(jnp.mean(x * x, axis=-1, keepdims=True) + EPS) * g


def _rope(z, cos, sin_lo, sin_hi):
    outs = []
    for s in range(z.shape[1] // LANES):
        xs = z[:, s * LANES:(s + 1) * LANES]
        nxt = pltpu.roll(xs, LANES - ROT_HALF, axis=1)
        prv = pltpu.roll(xs, ROT_HALF, axis=1)
        outs.append(xs * cos + nxt * sin_lo + prv * sin_hi)
    return jnp.concatenate(outs, axis=1)


def _inproj_kernel(x_ref, gpre_ref, w_ref, gv_ref, ws_ref, bs_ref, cos_ref, slo_ref, shi_ref,
                   *refs, gmlp_len, transposed, emit_vn, k_tile):
    if emit_vn:
        outa_ref, q_ref, k32_ref, kbf_ref, v32_ref, vbf_ref, vn_ref, hn_s, u_s = refs
    else:
        outa_ref, q_ref, k32_ref, kbf_ref, v32_ref, vbf_ref, hn_s, u_s = refs
        vn_ref = None
    j = pl.program_id(1)
    tm = x_ref.shape[0]

    @pl.when(j == 0)
    def _():
        hn_s[...] = _rms(x_ref[...], gpre_ref[...]).astype(BF16)

    z = jnp.dot(hn_s[...], w_ref[...], preferred_element_type=F32)

    @pl.when(j == 0)
    def _():
        u_s[...] = _gelu(z)

    @pl.when(j == 1)
    def _():
        vn = _rms(_gelu(z), gv_ref[...])
        if emit_vn:
            vn_ref[...] = vn
        vnb = vn.astype(BF16)
        ln = gmlp_len
        r = lax.broadcasted_iota(jnp.int32, (ln, ln), 0)
        c = lax.broadcasted_iota(jnp.int32, (ln, ln), 1)
        causal = c <= r
        for g in range(N_GROUPS):
            wg = jnp.where(causal, ws_ref[g], 0.0).astype(BF16)
            cols = slice(g * GROUP_DIM, (g + 1) * GROUP_DIM)
            bias = bs_ref[:, cols]
            for ch in range(tm // ln):
                rows = slice(ch * ln, (ch + 1) * ln)
                sg = jnp.dot(wg, vnb[rows, cols], preferred_element_type=F32) + bias
                outa_ref[rows, cols] = (u_s[rows, cols] * sg).astype(BF16)

    @pl.when(j == 2)
    def _():
        q = _rope(z, cos_ref[...], slo_ref[...], shi_ref[...]) * (ATT_SCALE * LOG2E)
        if transposed:
            q_ref[...] = q.T.astype(BF16)
        else:
            q_ref[...] = q.astype(BF16)

    @pl.when(j == 3)
    def _():
        k = _rope(z, cos_ref[...], slo_ref[...], shi_ref[...])
        k32_ref[...] = k
        kbf_ref[...] = k.astype(BF16)

    @pl.when(j == 4)
    def _():
        v32_ref[...] = z
        if transposed:
            vt = z.T.astype(BF16)
            for cb in range(tm // k_tile):
                vbf_ref[cb] = vt[:, cb * k_tile:(cb + 1) * k_tile]
        else:
            vbf_ref[...] = z.astype(BF16)


def _rope_tables(pos):
    n = pos.shape[0]
    inv_freq = ROPE_THETA ** (-jnp.arange(ROT_HALF, dtype=F32) / ROT_HALF)
    ang = pos.astype(F32)[:, None] * inv_freq[None, :]
    cos, sin = jnp.cos(ang), jnp.sin(ang)
    rest = HEAD_DIM - ROT_DIM
    cos64 = jnp.concatenate([cos, cos, jnp.ones((n, rest), F32)], axis=1)
    lo64 = jnp.concatenate([-sin, jnp.zeros((n, HEAD_DIM - ROT_HALF), F32)], axis=1)
    hi64 = jnp.concatenate([jnp.zeros((n, ROT_HALF), F32), sin, jnp.zeros((n, rest), F32)], axis=1)
    two = lambda t: jnp.concatenate([t, t], axis=1)
    return two(cos64), two(lo64), two(hi64)


def _inproj(x, pos, g_pre, w_in, gv, w_s, b_s, *, gmlp_len, transposed, emit_vn, tm):
    rows, d = x.shape
    assert rows % tm == 0 and tm % gmlp_len == 0
    k_tile = min(ATT_K_TILE, tm)
    cos, slo, shi = _rope_tables(pos)
    ws = w_s[:, :gmlp_len, :gmlp_len]
    bs = jnp.repeat(b_s[:, :gmlp_len].T, GROUP_DIM, axis=1)

    row_blk = lambda w, dt=None: pl.BlockSpec((tm, w), lambda i, j: (i, 0))
    const = lambda shape: pl.BlockSpec(shape, lambda i, j: (0,) * len(shape))
    in_specs = [
        row_blk(d), const((1, d)),
        pl.BlockSpec((d, GMLP_W), lambda i, j: (0, j)),
        const((1, GMLP_W)), const(ws.shape), const(bs.shape),
        row_blk(LANES), row_blk(LANES), row_blk(LANES),
    ]
    if transposed:
        q_shape, q_spec = (ATT_W, rows), pl.BlockSpec((ATT_W, tm), lambda i, j: (0, i))
        v_shape = (rows // k_tile, ATT_W, k_tile)
        v_spec = pl.BlockSpec((tm // k_tile, ATT_W, k_tile), lambda i, j: (i, 0, 0))
    else:
        q_shape, q_spec = (rows, ATT_W), row_blk(ATT_W)
        v_shape, v_spec = (rows, ATT_W), row_blk(ATT_W)
    out_shape = [
        jax.ShapeDtypeStruct((rows, GMLP_W), BF16), jax.ShapeDtypeStruct(q_shape, BF16),
        jax.ShapeDtypeStruct((rows, ATT_W), F32), jax.ShapeDtypeStruct((rows, ATT_W), BF16),
        jax.ShapeDtypeStruct((rows, ATT_W), F32), jax.ShapeDtypeStruct(v_shape, BF16),
    ]
    out_specs = [row_blk(GMLP_W), q_spec, row_blk(ATT_W), row_blk(ATT_W), row_blk(ATT_W), v_spec]
    if emit_vn:
        out_shape.append(jax.ShapeDtypeStruct((rows, GMLP_W), F32))
        out_specs.append(row_blk(GMLP_W))
    est = (2 * tm * d * 4 + 2 * d * GMLP_W * 2 + tm * d * 2 + tm * GMLP_W * 4
           + 2 * tm * GMLP_W * (2 + 2 + 4 + 2 + 4 + 2 + 4) + 8 * tm * GMLP_W * 4)
    return pl.pallas_call(
        functools.partial(_inproj_kernel, gmlp_len=gmlp_len, transposed=transposed,
                          emit_vn=emit_vn, k_tile=k_tile),
        grid=(rows // tm, N_PROJ_BLOCKS),
        in_specs=in_specs, out_specs=out_specs, out_shape=out_shape,
        scratch_shapes=[pltpu.VMEM((tm, d), BF16), pltpu.VMEM((tm, GMLP_W), F32)],
        compiler_params=pltpu.CompilerParams(
            dimension_semantics=("arbitrary", "arbitrary"), vmem_limit_bytes=_vmem_limit(est)),
        name="inproj",
    )(x, g_pre, w_in, gv, ws, bs, cos, slo, shi)


def _lambda(lq1_ref, lk1_ref, lq2_ref, lk2_ref, lam_init):
    d1 = jnp.sum(lq1_ref[...] * lk1_ref[...], axis=-1, keepdims=True)
    d2 = jnp.sum(lq2_ref[...] * lk2_ref[...], axis=-1, keepdims=True)
    return jnp.exp(d1) - jnp.exp(d2) + lam_init


def _prompt_attn_kernel(qt_ref, k_ref, vt_ref, lq1_ref, lk1_ref, lq2_ref, lk2_ref, g_ref, out_ref,
                        s_buf, smax_buf, p_buf, alpha_buf, m_s, l_s, acc_s, *, tq, tk, lam_init):
    qi = pl.program_id(1)
    qt = qt_ref[...]
    comp = lax.broadcasted_iota(jnp.int32, qt.shape, 0) < HEAD_DIM
    zero = jnp.zeros_like(qt)
    qs = (jnp.where(comp, qt, zero), jnp.where(comp, zero, qt))

    m_s[...] = jnp.full(m_s.shape, -jnp.inf, F32)
    l_s[...] = jnp.zeros(l_s.shape, F32)
    acc_s[...] = jnp.zeros(acc_s.shape, F32)

    steps_per_tile = tq // tk
    n_full = qi * steps_per_tile
    n_total = n_full + steps_per_tile

    def block_of(j):
        if isinstance(j, int):
            return n_full + j if j < steps_per_tile else j - steps_per_tile
        return jnp.where(j < steps_per_tile, n_full + j, j - steps_per_tile)

    def scores(j, slot):
        kb = block_of(j)
        kblk = k_ref[pl.ds(pl.multiple_of(kb * tk, tk), tk), :]
        masked = isinstance(j, int) and j < steps_per_tile
        if masked:
            key_chunk = (lax.broadcasted_iota(jnp.int32, (tk, tq), 0) + j * tk) // CHUNK
            allowed = key_chunk <= lax.broadcasted_iota(jnp.int32, (tk, tq), 1) // CHUNK
        for c in range(2):
            s = jnp.dot(kblk, qs[c], preferred_element_type=F32)
            if masked:
                s = jnp.where(allowed, s, -jnp.inf)
            s_buf[slot, c] = s
            smax_buf[slot, c] = jnp.max(s, axis=0, keepdims=True)

    def softmax(slot):
        for c in range(2):
            m_old = m_s[c]
            m_new = jnp.maximum(m_old, smax_buf[slot, c])
            alpha = jnp.exp2(m_old - m_new)
            alpha_buf[slot, c] = alpha
            p = jnp.exp2(s_buf[slot, c] - m_new)
            l_s[c] = alpha * l_s[c] + jnp.sum(p, axis=0, keepdims=True)
            p_buf[slot, c] = p.astype(BF16)
            m_s[c] = m_new

    def values(j, slot):
        vtb = vt_ref[block_of(j)]
        for c in range(2):
            acc_s[c] = alpha_buf[slot, c] * acc_s[c] + jnp.dot(
                vtb, p_buf[slot, c], preferred_element_type=F32)

    def half_step(t, slot):
        scores(t, slot)
        if not isinstance(t, int) or t >= 1:
            softmax(1 - slot)
        if not isinstance(t, int) or t >= 2:
            values(t - 2, slot)

    for t in range(steps_per_tile):
        half_step(t, t % 2)

    def body(u, carry):
        t = steps_per_tile + 2 * u
        half_step(t, 0)
        pl.when(qi >= 0)(lambda: half_step(t + 1, 1))
        return carry

    lax.fori_loop(0, n_full // 2, body, 0)
    softmax(1)
    values(n_total - 2, 0)
    values(n_total - 1, 1)

    lam = _lambda(lq1_ref, lk1_ref, lq2_ref, lk2_ref, lam_init)
    o = acc_s[0] / l_s[0] - lam * (acc_s[1] / l_s[1])
    on = o * lax.rsqrt(jnp.mean(o * o, axis=0, keepdims=True) + EPS)
    out_ref[...] = (on.T * (g_ref[...] * (1.0 - lam_init))).astype(BF16)


def _prompt_attention(qt, kbf, vt, lam_vecs, subln_g, *, lam_init, tq, tk):
    rows = kbf.shape[0]
    assert rows % tq == 0 and tq % (2 * tk) == 0 and tk % CHUNK == 0 and vt.shape[2] == tk
    vec = pl.BlockSpec((1, HEAD_DIM), lambda h, i: (0, 0))
    est = 2 * 2 * rows * HEAD_W * 2 + 2 * HEAD_W * tq * 4 + 12 * tk * tq * 4
    return pl.pallas_call(
        functools.partial(_prompt_attn_kernel, tq=tq, tk=tk, lam_init=lam_init),
        grid=(N_HEADS, rows // tq),
        in_specs=[
            pl.BlockSpec((HEAD_W, tq), lambda h, i: (h, i)),
            pl.BlockSpec((rows, HEAD_W), lambda h, i: (0, h)),
            pl.BlockSpec((rows // tk, HEAD_W, tk), lambda h, i: (0, h, 0)),
            vec, vec, vec, vec,
            pl.BlockSpec((1, HEAD_W), lambda h, i: (0, 0)),
        ],
        out_specs=pl.BlockSpec((tq, HEAD_W), lambda h, i: (i, h)),
        out_shape=jax.ShapeDtypeStruct((rows, ATT_W), BF16),
        scratch_shapes=[pltpu.VMEM((2, 2, tk, tq), F32), pltpu.VMEM((2, 2, 1, tq), F32),
                        pltpu.VMEM((2, 2, tk, tq), BF16), pltpu.VMEM((2, 2, 1, tq), F32),
                        pltpu.VMEM((2, 1, tq), F32), pltpu.VMEM((2, 1, tq), F32),
                        pltpu.VMEM((2, HEAD_W, tq), F32)],
        compiler_params=pltpu.CompilerParams(
            dimension_semantics=("arbitrary", "arbitrary"), vmem_limit_bytes=_vmem_limit(est)),
        name="prompt_attn",
    )(qt, kbf, vt, *lam_vecs, subln_g)


def _sample_attn_kernel(q_ref, kc_ref, vc_ref, kn_ref, vn_ref, lq1_ref, lk1_ref, lq2_ref, lk2_ref,
                        g_ref, out_ref, *, past_len, lam_init):
    q = q_ref[...]
    t = q.shape[0]
    comp = lax.broadcasted_iota(jnp.int32, q.shape, 1) < HEAD_DIM
    zero = jnp.zeros_like(q)
    qs = (jnp.where(comp, q, zero), jnp.where(comp, zero, q))
    kc = kc_ref[0].astype(BF16)
    vc = vc_ref[0].astype(BF16)
    kn = kn_ref[...]
    vn = vn_ref[...]
    q_chunk = (past_len + lax.broadcasted_iota(jnp.int32, (t, 1), 0)) // CHUNK
    ok_c = lax.broadcasted_iota(jnp.int32, (t, past_len), 1) // CHUNK <= q_chunk
    ok_n = (past_len + lax.broadcasted_iota(jnp.int32, (t, t), 1)) // CHUNK <= q_chunk
    nt = (((1,), (1,)), ((), ()))
    outs = []
    for c in range(2):
        sc = lax.dot_general(qs[c], kc, nt, preferred_element_type=F32)
        sn = lax.dot_general(qs[c], kn, nt, preferred_element_type=F32)
        sc = jnp.where(ok_c, sc, -jnp.inf)
        sn = jnp.where(ok_n, sn, -jnp.inf)
        m = jnp.maximum(jnp.max(sc, axis=-1, keepdims=True), jnp.max(sn, axis=-1, keepdims=True))
        pc = jnp.exp2(sc - m)
        pn = jnp.exp2(sn - m)
        den = jnp.sum(pc, axis=-1, keepdims=True) + jnp.sum(pn, axis=-1, keepdims=True)
        num = (jnp.dot(pc.astype(BF16), vc, preferred_element_type=F32)
               + jnp.dot(pn.astype(BF16), vn, preferred_element_type=F32))
        outs.append(num / den)
    lam = _lambda(lq1_ref, lk1_ref, lq2_ref, lk2_ref, lam_init)
    o = outs[0] - lam * outs[1]
    out_ref[...] = (_rms(o, g_ref[...]) * (1.0 - lam_init)).astype(BF16)


def _sample_attention(q, kbf, vbf, cache_k, cache_v, lam_vecs, subln_g, *, lam_init, t_new):
    n_streams, past_len = cache_k.shape[:2]
    vec = pl.BlockSpec((1, HEAD_DIM), lambda b, h: (0, 0))
    new_blk = pl.BlockSpec((t_new, HEAD_W), lambda b, h: (b, h))
    cache_blk = pl.BlockSpec((1, past_len, HEAD_W), lambda b, h: (b, 0, h))
    return pl.pallas_call(
        functools.partial(_sample_attn_kernel, past_len=past_len, lam_init=lam_init),
        grid=(n_streams, N_HEADS),
        in_specs=[new_blk, cache_blk, cache_blk, new_blk, new_blk, vec, vec, vec, vec,
                  pl.BlockSpec((1, HEAD_W), lambda b, h: (0, 0))],
        out_specs=new_blk,
        out_shape=jax.ShapeDtypeStruct((n_streams * t_new, ATT_W), BF16),
        compiler_params=pltpu.CompilerParams(dimension_semantics=("arbitrary", "arbitrary")),
        name="sample_attn",
    )(q, cache_k, cache_v, kbf, vbf, *lam_vecs, subln_g)


def _outproj_kernel(x_ref, a_ref, b_ref, w_ref, gpost_ref, gffn_ref, x1_ref, hn_ref):
    mix = (jnp.dot(a_ref[...], w_ref[:GMLP_W, :], preferred_element_type=F32)
           + jnp.dot(b_ref[...], w_ref[GMLP_W:, :], preferred_element_type=F32))
    x1 = x_ref[...] + _rms(mix, gpost_ref[...])
    x1_ref[...] = x1
    hn_ref[...] = _rms(x1, gffn_ref[...]).astype(BF16)


def _outproj(x, out_a, out_b, w_out, g_post, g_ffn, *, tm):
    rows, d = x.shape
    row_blk = lambda w: pl.BlockSpec((tm, w), lambda i: (i, 0))
    const = lambda shape: pl.BlockSpec(shape, lambda i: (0,) * len(shape))
    est = 2 * tm * d * (4 + 4 + 2) + 2 * 2 * tm * GMLP_W * 2 + 2 * w_out.size * 2 + 4 * tm * d * 4
    return pl.pallas_call(
        _outproj_kernel,
        grid=(rows // tm,),
        in_specs=[row_blk(d), row_blk(GMLP_W), row_blk(ATT_W), const(w_out.shape),
                  const((1, d)), const((1, d))],
        out_specs=[row_blk(d), row_blk(d)],
        out_shape=[jax.ShapeDtypeStruct((rows, d), F32), jax.ShapeDtypeStruct((rows, d), BF16)],
        compiler_params=pltpu.CompilerParams(
            dimension_semantics=("arbitrary",), vmem_limit_bytes=_vmem_limit(est)),
        name="outproj",
    )(x, out_a, out_b, w_out, g_post, g_ffn)


def _ffn_kernel(hn_ref, x1_ref, wg_ref, wv_ref, cwg_ref, cwv_ref, cbg_ref, cbv_ref, hg_ref, hv_ref,
                wd_ref, gpost_ref, y_ref, cg_ref, cv_ref, acc_s, carg_s, carv_s,
                *, n_streams, stream_len):
    i = pl.program_id(0)
    j = pl.program_id(1)
    tm = hn_ref.shape[0]
    ft = wg_ref.shape[1]
    hn = hn_ref[...]
    sub = min(FF_SUBTILE, ft)
    ridx = lax.broadcasted_iota(jnp.int32, (SUBLANES, sub), 0)

    @pl.when(j == 0)
    def _():
        acc_s[...] = jnp.zeros(acc_s.shape, F32)

    if n_streams == 1:
        @pl.when(i == 0)
        def _():
            pad = jnp.zeros((SUBLANES - (CONV_W - 1), ft), F32)
            carg_s[j] = jnp.concatenate([pad, hg_ref[0]], axis=0)
            carv_s[j] = jnp.concatenate([pad, hv_ref[0]], axis=0)

    def conv_branch(cs, w_ref, cw_ref, cb_ref, h_ref, car_s, cout_ref):
        up = jnp.dot(hn, w_ref[:, cs], preferred_element_type=F32)
        if n_streams == 1:
            tail = car_s[j][:, cs]
            hists = [(tail[SUBLANES - 2:SUBLANES - 1], tail[SUBLANES - 1:SUBLANES])]
        else:
            hists = [(h_ref[s, 0:1, cs], h_ref[s, 1:2, cs]) for s in range(n_streams)]
        r1 = pltpu.roll(up, 1, axis=0)
        r2 = pltpu.roll(up, 2, axis=0)
        p1, p2 = [], []
        for s, (h0, h1) in enumerate(hists):
            lo = s * stream_len
            p1.append(jnp.where(ridx == 0, h1, r1[lo:lo + SUBLANES]))
            p1.append(r1[lo + SUBLANES:lo + stream_len])
            p2.append(jnp.where(ridx == 0, h0, jnp.where(ridx == 1, h1, r2[lo:lo + SUBLANES])))
            p2.append(r2[lo + SUBLANES:lo + stream_len])
            cout_ref[0, s, :, cs] = up[lo + stream_len - (CONV_W - 1):lo + stream_len]
        prev1 = jnp.concatenate(p1, axis=0)
        prev2 = jnp.concatenate(p2, axis=0)
        if n_streams == 1:
            car_s[j, :, cs] = up[tm - SUBLANES:tm]
        cw = cw_ref[:, cs]
        return prev2 * cw[0:1] + prev1 * cw[1:2] + up * cw[2:3] + cb_ref[:, cs]

    part = None
    for c0 in range(0, ft, sub):
        cs = slice(c0, c0 + sub)
        gate = conv_branch(cs, wg_ref, cwg_ref, cbg_ref, hg_ref, carg_s, cg_ref)
        val = conv_branch(cs, wv_ref, cwv_ref, cbv_ref, hv_ref, carv_s, cv_ref)
        act = (_gelu(gate) * val).astype(BF16)
        d_part = jnp.dot(act, wd_ref[cs, :], preferred_element_type=F32)
        part = d_part if part is None else part + d_part
    acc_s[...] += part

    @pl.when(j == pl.num_programs(1) - 1)
    def _():
        y_ref[...] = x1_ref[...] + _rms(acc_s[...], gpost_ref[...])


def _ffn(hn, x1, w_up, conv_w, conv_b, hist, w_down, g_post, *, tm, ft, n_streams):
    rows, d = x1.shape
    d_ff = w_down.shape[0]
    nj = d_ff // ft
    n_tiles = rows // tm
    stream_len = tm // n_streams
    assert rows % tm == 0 and d_ff % ft == 0 and (n_streams == 1 or n_tiles == 1)
    assert hist.shape == (n_streams, CONV_W - 1, 2 * d_ff) and stream_len % SUBLANES == 0
    gate_col = lambda i, j: (0, j)
    val_col = lambda i, j: (0, nj + j)
    row_blk = lambda w: pl.BlockSpec((tm, w), lambda i, j: (i, 0))
    hist_blk = lambda off: pl.BlockSpec((n_streams, CONV_W - 1, ft), lambda i, j: (0, 0, off + j))
    tail_blk = pl.BlockSpec((1, n_streams, CONV_W - 1, ft), lambda i, j: (i, 0, 0, j))
    tail_shape = jax.ShapeDtypeStruct((n_tiles, n_streams, CONV_W - 1, d_ff), F32)
    est = (2 * tm * d * (2 + 4 + 4) + tm * d * 4 + 2 * 3 * d * ft * 2 + 10 * tm * ft * 4
           + 2 * tm * d * 4)
    y, cg, cv = pl.pallas_call(
        functools.partial(_ffn_kernel, n_streams=n_streams, stream_len=stream_len),
        grid=(n_tiles, nj),
        in_specs=[
            row_blk(d), row_blk(d),
            pl.BlockSpec((d, ft), gate_col), pl.BlockSpec((d, ft), val_col),
            pl.BlockSpec((CONV_W, ft), gate_col), pl.BlockSpec((CONV_W, ft), val_col),
            pl.BlockSpec((1, ft), gate_col), pl.BlockSpec((1, ft), val_col),
            hist_blk(0), hist_blk(nj),
            pl.BlockSpec((ft, d), lambda i, j: (j, 0)),
            pl.BlockSpec((1, d), lambda i, j: (0, 0)),
        ],
        out_specs=[row_blk(d), tail_blk, tail_blk],
        out_shape=[jax.ShapeDtypeStruct((rows, d), F32), tail_shape, tail_shape],
        scratch_shapes=[pltpu.VMEM((tm, d), F32),
                        pltpu.VMEM((nj, SUBLANES, ft), F32), pltpu.VMEM((nj, SUBLANES, ft), F32)],
        compiler_params=pltpu.CompilerParams(
            dimension_semantics=("arbitrary", "arbitrary"), vmem_limit_bytes=_vmem_limit(est)),
        name="convffn",
    )(hn, x1, w_up, w_up, conv_w, conv_w, conv_b, conv_b, hist, hist, w_down, g_post)
    return y, jnp.concatenate([cg[-1], cv[-1]], axis=-1)


def _layer(x, pos, hist, attend, lw, *, gmlp_len, transposed, emit_vn, n_streams, tm):
    (g_mix_pre, w_in, gv, w_s, b_s, subln_g, w_out, g_mix_post, g_ffn_pre, w_up, conv_w, conv_b,
     w_down, g_ffn_post) = lw
    res = _inproj(x, pos, g_mix_pre, w_in, gv, w_s, b_s, gmlp_len=gmlp_len, transposed=transposed,
                  emit_vn=emit_vn, tm=tm)
    out_a, q, k32, kbf, v32, vbf = res[:6]
    vn = res[6] if emit_vn else None
    out_b = attend(q, kbf, vbf)
    x1, hn2 = _outproj(x, out_a, out_b, w_out, g_mix_post, g_ffn_pre, tm=tm)
    y, new_hist = _ffn(hn2, x1, w_up, conv_w, conv_b, hist, w_down, g_ffn_post,
                       tm=tm, ft=FF_TILE, n_streams=n_streams)
    return y, k32, v32, vn, new_hist


def kernel(x_prompt, x_sample, cache_k, cache_v, state_conv, g_mix_pre, w_in, gmlp_v_gain, gmlp_w_s,
           gmlp_b_s, lambda_q1, lambda_k1, lambda_q2, lambda_k2, subln_g, w_out, g_mix_post,
           g_ffn_pre, w_up, conv_w, conv_b, w_down, g_ffn_post):
    b_p, s_p, d = x_prompt.shape
    b_s, t_new, _ = x_sample.shape
    depth = w_in.shape[0]
    past_len = cache_k.shape[2]
    d_ff = w_down.shape[1]
    assert b_p == 1
    pos_p = jnp.arange(s_p)
    pos_s = jnp.tile(past_len + jnp.arange(t_new), b_s)
    zero_hist = jnp.zeros((b_p, CONV_W - 1, 2 * d_ff), F32)
    row = lambda a: a.reshape(1, -1)

    yp = x_prompt.reshape(b_p * s_p, d)
    ys = x_sample.reshape(b_s * t_new, d)
    outs = [[] for _ in range(7)]
    for l in range(depth):
        lam_init = 0.8 - 0.6 * math.exp(-0.3 * l)
        lam_vecs = tuple(row(a[l]) for a in (lambda_q1, lambda_k1, lambda_q2, lambda_k2))
        sub_g = row(subln_g[l])
        lw = (row(g_mix_pre[l]), w_in[l].astype(BF16), row(gmlp_v_gain[l]), gmlp_w_s[l], gmlp_b_s[l],
              sub_g, w_out[l].astype(BF16), row(g_mix_post[l]), row(g_ffn_pre[l]),
              w_up[l].astype(BF16), conv_w[l], row(conv_b[l]), w_down[l].astype(BF16),
              row(g_ffn_post[l]))

        tq = min(ATT_Q_TILE, s_p)
        attend_p = lambda q, k, v: _prompt_attention(
            q, k, v, lam_vecs, sub_g, lam_init=lam_init, tq=tq, tk=min(ATT_K_TILE, tq))
        yp, kp, vp, _, cp = _layer(yp, pos_p, zero_hist, attend_p, lw, gmlp_len=GROUP_DIM,
                                   transposed=True, emit_vn=False, n_streams=1,
                                   tm=min(ROW_TILE, s_p))

        ck = cache_k[l].reshape(b_s, past_len, ATT_W)
        cv = cache_v[l].reshape(b_s, past_len, ATT_W)
        attend_s = lambda q, k, v: _sample_attention(
            q, k, v, ck, cv, lam_vecs, sub_g, lam_init=lam_init, t_new=t_new)
        ys, ks, vs, gs, cs = _layer(ys, pos_s, state_conv[l], attend_s, lw, gmlp_len=t_new,
                                    transposed=False, emit_vn=True, n_streams=b_s,
                                    tm=b_s * t_new)

        for acc, val in zip(outs, (
                kp.reshape(b_p, s_p, N_HEADS, 2, HEAD_DIM), vp.reshape(b_p, s_p, N_HEADS, HEAD_W), cp,
                ks.reshape(b_s, t_new, N_HEADS, 2, HEAD_DIM), vs.reshape(b_s, t_new, N_HEADS, HEAD_W),
                gs.reshape(b_s, t_new, N_GROUPS, GROUP_DIM), cs)):
            acc.append(val)

    return (yp.reshape(b_p, s_p, d), ys.reshape(b_s, t_new, d)) + tuple(jnp.stack(o) for o in outs)
```

```python
import functools
import math

import jax
import jax.numpy as jnp
from jax import lax
from jax.experimental import pallas as pl
from jax.experimental.pallas import tpu as pltpu

F32 = jnp.float32
BF16 = jnp.bfloat16

N_HEADS = 8
HEAD_DIM = 64
HEAD_W = 2 * HEAD_DIM
CHUNK = 64
ROT_DIM = HEAD_DIM // 4
ROT_HALF = ROT_DIM // 2
ROPE_THETA = 500000.0
ATT_SCALE = HEAD_DIM ** -0.5
N_GROUPS = 8
GROUP_DIM = 128
GMLP_W = N_GROUPS * GROUP_DIM
ATT_W = N_HEADS * HEAD_W
CONV_W = 3
EPS = 1e-6
LOG2E = 1.4426950408889634

LANES = 128
SUBLANES = 8
VMEM_BYTES_V7X = 64 * 1024 * 1024

ROW_TILE = 512
FF_TILE = 512
FF_SUBTILE = 256
ATT_Q_TILE = 512
ATT_K_TILE = 512
N_PROJ_BLOCKS = 5


def _vmem_limit(estimate_bytes):
    return int(min(VMEM_BYTES_V7X - (4 << 20), max(32 << 20, estimate_bytes)))


def _gelu(x):
    a = -2.0 * math.sqrt(2.0 / math.pi) * LOG2E
    return x / (1.0 + jnp.exp2(x * (a + (a * 0.044715) * (x * x))))


def _rms(x, g):
    return x * lax.rsqrt(jnp.mean(x * x, axis=-1, keepdims=True) + EPS) * g


def _rope(z, cos, sin_lo, sin_hi):
    outs = []
    for s in range(z.shape[1] // LANES):
        xs = z[:, s * LANES:(s + 1) * LANES]
        nxt = pltpu.roll(xs, LANES - ROT_HALF, axis=1)
        prv = pltpu.roll(xs, ROT_HALF, axis=1)
        outs.append(xs * cos + nxt * sin_lo + prv * sin_hi)
    return jnp.concatenate(outs, axis=1)


def _inproj_kernel(x_ref, gpre_ref, w_ref, gv_ref, ws_ref, bs_ref, cos_ref, slo_ref, shi_ref,
                   *refs, gmlp_len, transposed, emit_vn, k_tile):
    if emit_vn:
        outa_ref, q_ref, k32_ref, kbf_ref, v32_ref, vbf_ref, vn_ref, hn_s, u_s = refs
    else:
        outa_ref, q_ref, k32_ref, kbf_ref, v32_ref, vbf_ref, hn_s, u_s = refs
        vn_ref = None
    j = pl.program_id(1)
    tm = x_ref.shape[0]

    @pl.when(j == 0)
    def _():
        hn_s[...] = _rms(x_ref[...], gpre_ref[...]).astype(BF16)

    z = jnp.dot(hn_s[...], w_ref[...], preferred_element_type=F32)

    @pl.when(j == 0)
    def _():
        u_s[...] = _gelu(z)

    @pl.when(j == 1)
    def _():
        vn = _rms(_gelu(z), gv_ref[...])
        if emit_vn:
            vn_ref[...] = vn
        vnb = vn.astype(BF16)
        ln = gmlp_len
        r = lax.broadcasted_iota(jnp.int32, (ln, ln), 0)
        c = lax.broadcasted_iota(jnp.int32, (ln, ln), 1)
        causal = c <= r
        for g in range(N_GROUPS):
            wg = jnp.where(causal, ws_ref[g], 0.0).astype(BF16)
            cols = slice(g * GROUP_DIM, (g + 1) * GROUP_DIM)
            bias = bs_ref[:, cols]
            for ch in range(tm // ln):
                rows = slice(ch * ln, (ch + 1) * ln)
                sg = jnp.dot(wg, vnb[rows, cols], preferred_element_type=F32) + bias
                outa_ref[rows, cols] = (u_s[rows, cols] * sg).astype(BF16)

    @pl.when(j == 2)
    def _():
        q = _rope(z, cos_ref[...], slo_ref[...], shi_ref[...]) * (ATT_SCALE * LOG2E)
        if transposed:
            q_ref[...] = q.T.astype(BF16)
        else:
            q_ref[...] = q.astype(BF16)

    @pl.when(j == 3)
    def _():
        k = _rope(z, cos_ref[...], slo_ref[...], shi_ref[...])
        k32_ref[...] = k
        kbf_ref[...] = k.astype(BF16)

    @pl.when(j == 4)
    def _():
        v32_ref[...] = z
        if transposed:
            vt = z.T.astype(BF16)
            for cb in range(tm // k_tile):
                vbf_ref[cb] = vt[:, cb * k_tile:(cb + 1) * k_tile]
        else:
            vbf_ref[...] = z.astype(BF16)


def _rope_tables(pos):
    n = pos.shape[0]
    inv_freq = ROPE_THETA ** (-jnp.arange(ROT_HALF, dtype=F32) / ROT_HALF)
    ang = pos.astype(F32)[:, None] * inv_freq[None, :]
    cos, sin = jnp.cos(ang), jnp.sin(ang)
    rest = HEAD_DIM - ROT_DIM
    cos64 = jnp.concatenate([cos, cos, jnp.ones((n, rest), F32)], axis=1)
    lo64 = jnp.concatenate([-sin, jnp.zeros((n, HEAD_DIM - ROT_HALF), F32)], axis=1)
    hi64 = jnp.concatenate([jnp.zeros((n, ROT_HALF), F32), sin, jnp.zeros((n, rest), F32)], axis=1)
    two = lambda t: jnp.concatenate([t, t], axis=1)
    return two(cos64), two(lo64), two(hi64)


def _inproj(x, pos, g_pre, w_in, gv, w_s, b_s, *, gmlp_len, transposed, emit_vn, tm):
    rows, d = x.shape
    assert rows % tm == 0 and tm % gmlp_len == 0
    k_tile = min(ATT_K_TILE, tm)
    cos, slo, shi = _rope_tables(pos)
    ws = w_s[:, :gmlp_len, :gmlp_len]
    bs = jnp.repeat(b_s[:, :gmlp_len].T, GROUP_DIM, axis=1)

    row_blk = lambda w, dt=None: pl.BlockSpec((tm, w), lambda i, j: (i, 0))
    const = lambda shape: pl.BlockSpec(shape, lambda i, j: (0,) * len(shape))
    in_specs = [
        row_blk(d), const((1, d)),
        pl.BlockSpec((d, GMLP_W), lambda i, j: (0, j)),
        const((1, GMLP_W)), const(ws.shape), const(bs.shape),
        row_blk(LANES), row_blk(LANES), row_blk(LANES),
    ]
    if transposed:
        q_shape, q_spec = (ATT_W, rows), pl.BlockSpec((ATT_W, tm), lambda i, j: (0, i))
        v_shape = (rows // k_tile, ATT_W, k_tile)
        v_spec = pl.BlockSpec((tm // k_tile, ATT_W, k_tile), lambda i, j: (i, 0, 0))
    else:
        q_shape, q_spec = (rows, ATT_W), row_blk(ATT_W)
        v_shape, v_spec = (rows, ATT_W), row_blk(ATT_W)
    out_shape = [
        jax.ShapeDtypeStruct((rows, GMLP_W), BF16), jax.ShapeDtypeStruct(q_shape, BF16),
        jax.ShapeDtypeStruct((rows, ATT_W), F32), jax.ShapeDtypeStruct((rows, ATT_W), BF16),
        jax.ShapeDtypeStruct((rows, ATT_W), F32), jax.ShapeDtypeStruct(v_shape, BF16),
    ]
    out_specs = [row_blk(GMLP_W), q_spec, row_blk(ATT_W), row_blk(ATT_W), row_blk(ATT_W), v_spec]
    if emit_vn:
        out_shape.append(jax.ShapeDtypeStruct((rows, GMLP_W), F32))
        out_specs.append(row_blk(GMLP_W))
    est = (2 * tm * d * 4 + 2 * d * GMLP_W * 2 + tm * d * 2 + tm * GMLP_W * 4
           + 2 * tm * GMLP_W * (2 + 2 + 4 + 2 + 4 + 2 + 4) + 8 * tm * GMLP_W * 4)
    return pl.pallas_call(
        functools.partial(_inproj_kernel, gmlp_len=gmlp_len, transposed=transposed,
                          emit_vn=emit_vn, k_tile=k_tile),
        grid=(rows // tm, N_PROJ_BLOCKS),
        in_specs=in_specs, out_specs=out_specs, out_shape=out_shape,
        scratch_shapes=[pltpu.VMEM((tm, d), BF16), pltpu.VMEM((tm, GMLP_W), F32)],
        compiler_params=pltpu.CompilerParams(
            dimension_semantics=("arbitrary", "arbitrary"), vmem_limit_bytes=_vmem_limit(est)),
        name="inproj",
    )(x, g_pre, w_in, gv, ws, bs, cos, slo, shi)


def _lambda(lq1_ref, lk1_ref, lq2_ref, lk2_ref, lam_init):
    d1 = jnp.sum(lq1_ref[...] * lk1_ref[...], axis=-1, keepdims=True)
    d2 = jnp.sum(lq2_ref[...] * lk2_ref[...], axis=-1, keepdims=True)
    return jnp.exp(d1) - jnp.exp(d2) + lam_init


def _prompt_attn_kernel(qt_ref, k_ref, vt_ref, lq1_ref, lk1_ref, lq2_ref, lk2_ref, g_ref, out_ref,
                        s_buf, smax_buf, p_buf, alpha_buf, m_s, l_s, acc_s, *, tq, tk, lam_init):
    qi = pl.program_id(1)
    qt = qt_ref[...]
    comp = lax.broadcasted_iota(jnp.int32, qt.shape, 0) < HEAD_DIM
    zero = jnp.zeros_like(qt)
    qs = (jnp.where(comp, qt, zero), jnp.where(comp, zero, qt))

    m_s[...] = jnp.full(m_s.shape, -jnp.inf, F32)
    l_s[...] = jnp.zeros(l_s.shape, F32)
    acc_s[...] = jnp.zeros(acc_s.shape, F32)

    steps_per_tile = tq // tk
    n_full = qi * steps_per_tile
    n_total = n_full + steps_per_tile

    def block_of(j):
        if isinstance(j, int):
            return n_full + j if j < steps_per_tile else j - steps_per_tile
        return jnp.where(j < steps_per_tile, n_full + j, j - steps_per_tile)

    def scores(j):
        kb = block_of(j)
        kblk = k_ref[pl.ds(pl.multiple_of(kb * tk, tk), tk), :]
        masked = isinstance(j, int) and j < steps_per_tile
        if masked:
            key_chunk = (lax.broadcasted_iota(jnp.int32, (tk, tq), 0) + j * tk) // CHUNK
            allowed = key_chunk <= lax.broadcasted_iota(jnp.int32, (tk, tq), 1) // CHUNK
        for c in range(2):
            s = jnp.dot(kblk, qs[c], preferred_element_type=F32)
            if masked:
                s = jnp.where(allowed, s, -jnp.inf)
            s_buf[c] = s
            smax_buf[c] = jnp.max(s, axis=0, keepdims=True)

    def softmax():
        for c in range(2):
            m_old = m_s[c]
            m_new = jnp.maximum(m_old, smax_buf[c])
            alpha = jnp.exp2(m_old - m_new)
            alpha_buf[c] = alpha
            p = jnp.exp2(s_buf[c] - m_new)
            l_s[c] = alpha * l_s[c] + jnp.sum(p, axis=0, keepdims=True)
            p_buf[c] = p.astype(BF16)
            m_s[c] = m_new

    def values(j):
        vtb = vt_ref[block_of(j)]
        for c in range(2):
            acc_s[c] = alpha_buf[c] * acc_s[c] + jnp.dot(vtb, p_buf[c], preferred_element_type=F32)

    def step(u, with_values=True, with_softmax=True):
        if with_values:
            values(u - 2)
        if with_softmax:
            softmax()
        scores(u)

    def body(u, carry):
        step(u)
        return carry

    def drain():
        values(n_total - 2)
        softmax()

    if steps_per_tile >= 2:
        for u in range(steps_per_tile):
            step(u, with_values=u >= 2, with_softmax=u >= 1)
        lax.fori_loop(steps_per_tile, n_total, body, 0)
        drain()
    else:
        step(0, with_values=False, with_softmax=False)

        @pl.when(qi > 0)
        def _():
            step(1, with_values=False)
            lax.fori_loop(2, n_total, body, 0)
            drain()

        pl.when(qi == 0)(softmax)
    values(n_total - 1)

    lam = _lambda(lq1_ref, lk1_ref, lq2_ref, lk2_ref, lam_init)
    o = acc_s[0] / l_s[0] - lam * (acc_s[1] / l_s[1])
    on = o * lax.rsqrt(jnp.mean(o * o, axis=0, keepdims=True) + EPS)
    out_ref[...] = (on.T * (g_ref[...] * (1.0 - lam_init))).astype(BF16)


def _prompt_attention(qt, kbf, vt, lam_vecs, subln_g, *, lam_init, tq, tk):
    rows = kbf.shape[0]
    assert rows % tq == 0 and tq % tk == 0 and tk % CHUNK == 0 and vt.shape[2] == tk
    vec = pl.BlockSpec((1, HEAD_DIM), lambda h, i: (0, 0))
    est = 2 * 2 * rows * HEAD_W * 2 + 2 * HEAD_W * tq * 4 + 12 * tk * tq * 4
    return pl.pallas_call(
        functools.partial(_prompt_attn_kernel, tq=tq, tk=tk, lam_init=lam_init),
        grid=(N_HEADS, rows // tq),
        in_specs=[
            pl.BlockSpec((HEAD_W, tq), lambda h, i: (h, i)),
            pl.BlockSpec((rows, HEAD_W), lambda h, i: (0, h)),
            pl.BlockSpec((rows // tk, HEAD_W, tk), lambda h, i: (0, h, 0)),
            vec, vec, vec, vec,
            pl.BlockSpec((1, HEAD_W), lambda h, i: (0, 0)),
        ],
        out_specs=pl.BlockSpec((tq, HEAD_W), lambda h, i: (i, h)),
        out_shape=jax.ShapeDtypeStruct((rows, ATT_W), BF16),
        scratch_shapes=[pltpu.VMEM((2, tk, tq), F32), pltpu.VMEM((2, 1, tq), F32),
                        pltpu.VMEM((2, tk, tq), BF16), pltpu.VMEM((2, 1, tq), F32),
                        pltpu.VMEM((2, 1, tq), F32), pltpu.VMEM((2, 1, tq), F32),
                        pltpu.VMEM((2, HEAD_W, tq), F32)],
        compiler_params=pltpu.CompilerParams(
            dimension_semantics=("arbitrary", "arbitrary"), vmem_limit_bytes=_vmem_limit(est)),
        name="prompt_attn",
    )(qt, kbf, vt, *lam_vecs, subln_g)


def _sample_attn_kernel(q_ref, kc_ref, vc_ref, kn_ref, vn_ref, lq1_ref, lk1_ref, lq2_ref, lk2_ref,
                        g_ref, out_ref, *, past_len, lam_init):
    q = q_ref[...]
    t = q.shape[0]
    comp = lax.broadcasted_iota(jnp.int32, q.shape, 1) < HEAD_DIM
    zero = jnp.zeros_like(q)
    qs = (jnp.where(comp, q, zero), jnp.where(comp, zero, q))
    kc = kc_ref[0].astype(BF16)
    vc = vc_ref[0].astype(BF16)
    kn = kn_ref[...]
    vn = vn_ref[...]
    q_chunk = (past_len + lax.broadcasted_iota(jnp.int32, (t, 1), 0)) // CHUNK
    ok_c = lax.broadcasted_iota(jnp.int32, (t, past_len), 1) // CHUNK <= q_chunk
    ok_n = (past_len + lax.broadcasted_iota(jnp.int32, (t, t), 1)) // CHUNK <= q_chunk
    nt = (((1,), (1,)), ((), ()))
    outs = []
    for c in range(2):
        sc = lax.dot_general(qs[c], kc, nt, preferred_element_type=F32)
        sn = lax.dot_general(qs[c], kn, nt, preferred_element_type=F32)
        sc = jnp.where(ok_c, sc, -jnp.inf)
        sn = jnp.where(ok_n, sn, -jnp.inf)
        m = jnp.maximum(jnp.max(sc, axis=-1, keepdims=True), jnp.max(sn, axis=-1, keepdims=True))
        pc = jnp.exp2(sc - m)
        pn = jnp.exp2(sn - m)
        den = jnp.sum(pc, axis=-1, keepdims=True) + jnp.sum(pn, axis=-1, keepdims=True)
        num = (jnp.dot(pc.astype(BF16), vc, preferred_element_type=F32)
               + jnp.dot(pn.astype(BF16), vn, preferred_element_type=F32))
        outs.append(num / den)
    lam = _lambda(lq1_ref, lk1_ref, lq2_ref, lk2_ref, lam_init)
    o = outs[0] - lam * outs[1]
    out_ref[...] = (_rms(o, g_ref[...]) * (1.0 - lam_init)).astype(BF16)


def _sample_attention(q, kbf, vbf, cache_k, cache_v, lam_vecs, subln_g, *, lam_init, t_new):
    n_streams, past_len = cache_k.shape[:2]
    vec = pl.BlockSpec((1, HEAD_DIM), lambda b, h: (0, 0))
    new_blk = pl.BlockSpec((t_new, HEAD_W), lambda b, h: (b, h))
    cache_blk = pl.BlockSpec((1, past_len, HEAD_W), lambda b, h: (b, 0, h))
    return pl.pallas_call(
        functools.partial(_sample_attn_kernel, past_len=past_len, lam_init=lam_init),
        grid=(n_streams, N_HEADS),
        in_specs=[new_blk, cache_blk, cache_blk, new_blk, new_blk, vec, vec, vec, vec,
                  pl.BlockSpec((1, HEAD_W), lambda b, h: (0, 0))],
        out_specs=new_blk,
        out_shape=jax.ShapeDtypeStruct((n_streams * t_new, ATT_W), BF16),
        compiler_params=pltpu.CompilerParams(dimension_semantics=("arbitrary", "arbitrary")),
        name="sample_attn",
    )(q, cache_k, cache_v, kbf, vbf, *lam_vecs, subln_g)


def _outproj_kernel(x_ref, a_ref, b_ref, w_ref, gpost_ref, gffn_ref, x1_ref, hn_ref):
    mix = (jnp.dot(a_ref[...], w_ref[:GMLP_W, :], preferred_element_type=F32)
           + jnp.dot(b_ref[...], w_ref[GMLP_W:, :], preferred_element_type=F32))
    x1 = x_ref[...] + _rms(mix, gpost_ref[...])
    x1_ref[...] = x1
    hn_ref[...] = _rms(x1, gffn_ref[...]).astype(BF16)


def _outproj(x, out_a, out_b, w_out, g_post, g_ffn, *, tm):
    rows, d = x.shape
    row_blk = lambda w: pl.BlockSpec((tm, w), lambda i: (i, 0))
    const = lambda shape: pl.BlockSpec(shape, lambda i: (0,) * len(shape))
    est = 2 * tm * d * (4 + 4 + 2) + 2 * 2 * tm * GMLP_W * 2 + 2 * w_out.size * 2 + 4 * tm * d * 4
    return pl.pallas_call(
        _outproj_kernel,
        grid=(rows // tm,),
        in_specs=[row_blk(d), row_blk(GMLP_W), row_blk(ATT_W), const(w_out.shape),
                  const((1, d)), const((1, d))],
        out_specs=[row_blk(d), row_blk(d)],
        out_shape=[jax.ShapeDtypeStruct((rows, d), F32), jax.ShapeDtypeStruct((rows, d), BF16)],
        compiler_params=pltpu.CompilerParams(
            dimension_semantics=("arbitrary",), vmem_limit_bytes=_vmem_limit(est)),
        name="outproj",
    )(x, out_a, out_b, w_out, g_post, g_ffn)


def _ffn_kernel(hn_ref, x1_ref, wg_ref, wv_ref, cwg_ref, cwv_ref, cbg_ref, cbv_ref, hg_ref, hv_ref,
                wd_ref, gpost_ref, y_ref, cg_ref, cv_ref, acc_s, carg_s, carv_s, stage_s,
                *, n_streams, stream_len):
    i = pl.program_id(0)
    j = pl.program_id(1)
    tm = hn_ref.shape[0]
    ft = wg_ref.shape[1]
    hn = hn_ref[...]
    sub = min(FF_SUBTILE, ft)

    @pl.when(j == 0)
    def _():
        acc_s[...] = jnp.zeros(acc_s.shape, F32)

    if n_streams == 1:
        @pl.when(i == 0)
        def _():
            pad = jnp.zeros((SUBLANES - (CONV_W - 1), ft), F32)
            carg_s[j] = jnp.concatenate([pad, hg_ref[0]], axis=0)
            carv_s[j] = jnp.concatenate([pad, hv_ref[0]], axis=0)

    def conv_branch(buf, cs, w_ref, cw_ref, cb_ref, h_ref, car_s, cout_ref):
        up = jnp.dot(hn, w_ref[:, cs], preferred_element_type=F32)
        taps = [[], [], []]
        for s in range(n_streams):
            lo = s * stream_len
            if n_streams == 1:
                stage_s[buf, s, 0:SUBLANES] = car_s[j][:, cs]
            else:
                stage_s[buf, s, SUBLANES - (CONV_W - 1):SUBLANES] = h_ref[s, :, cs]
            stage_s[buf, s, SUBLANES:SUBLANES + stream_len] = up[lo:lo + stream_len]
            for k in range(CONV_W):
                first = SUBLANES - (CONV_W - 1) + k
                taps[k].append(stage_s[buf, s, first:first + stream_len])
            cout_ref[0, s, :, cs] = stage_s[buf, s, SUBLANES + stream_len - (CONV_W - 1):
                                            SUBLANES + stream_len]
        if n_streams == 1:
            car_s[j, :, cs] = stage_s[buf, 0, stream_len:stream_len + SUBLANES]
        taps = [t[0] if n_streams == 1 else jnp.concatenate(t, axis=0) for t in taps]
        cw = cw_ref[:, cs]
        return taps[0] * cw[0:1] + taps[1] * cw[1:2] + taps[2] * cw[2:3] + cb_ref[:, cs]

    part = None
    for c0 in range(0, ft, sub):
        cs = slice(c0, c0 + sub)
        buf = 2 * (c0 // sub)
        gate = conv_branch(buf, cs, wg_ref, cwg_ref, cbg_ref, hg_ref, carg_s, cg_ref)
        val = conv_branch(buf + 1, cs, wv_ref, cwv_ref, cbv_ref, hv_ref, carv_s, cv_ref)
        act = (_gelu(gate) * val).astype(BF16)
        d_part = jnp.dot(act, wd_ref[cs, :], preferred_element_type=F32)
        part = d_part if part is None else part + d_part
    acc_s[...] += part

    @pl.when(j == pl.num_programs(1) - 1)
    def _():
        y_ref[...] = x1_ref[...] + _rms(acc_s[...], gpost_ref[...])


def _ffn(hn, x1, w_up, conv_w, conv_b, hist, w_down, g_post, *, tm, ft, n_streams):
    rows, d = x1.shape
    d_ff = w_down.shape[0]
    nj = d_ff // ft
    n_tiles = rows // tm
    stream_len = tm // n_streams
    sub = min(FF_SUBTILE, ft)
    assert rows % tm == 0 and d_ff % ft == 0 and (n_streams == 1 or n_tiles == 1)
    assert hist.shape == (n_streams, CONV_W - 1, 2 * d_ff) and stream_len % SUBLANES == 0
    gate_col = lambda i, j: (0, j)
    val_col = lambda i, j: (0, nj + j)
    row_blk = lambda w: pl.BlockSpec((tm, w), lambda i, j: (i, 0))
    once_per_tile = lambda w: pl.BlockSpec((tm, w), lambda i, j: (i, 0), pipeline_mode=pl.Buffered(1))
    hist_blk = lambda off: pl.BlockSpec((n_streams, CONV_W - 1, ft), lambda i, j: (0, 0, off + j))
    tail_blk = pl.BlockSpec((1, n_streams, CONV_W - 1, ft), lambda i, j: (i, 0, 0, j))
    tail_shape = jax.ShapeDtypeStruct((n_tiles, n_streams, CONV_W - 1, d_ff), F32)
    est = (tm * d * (2 + 4 + 2 * 4) + tm * d * 4 + 2 * 3 * d * ft * 2 + 16 * tm * FF_SUBTILE * 4
           + 2 * tm * d * 4)
    y, cg, cv = pl.pallas_call(
        functools.partial(_ffn_kernel, n_streams=n_streams, stream_len=stream_len),
        grid=(n_tiles, nj),
        in_specs=[
            once_per_tile(d), once_per_tile(d),
            pl.BlockSpec((d, ft), gate_col), pl.BlockSpec((d, ft), val_col),
            pl.BlockSpec((CONV_W, ft), gate_col), pl.BlockSpec((CONV_W, ft), val_col),
            pl.BlockSpec((1, ft), gate_col), pl.BlockSpec((1, ft), val_col),
            hist_blk(0), hist_blk(nj),
            pl.BlockSpec((ft, d), lambda i, j: (j, 0)),
            pl.BlockSpec((1, d), lambda i, j: (0, 0)),
        ],
        out_specs=[row_blk(d), tail_blk, tail_blk],
        out_shape=[jax.ShapeDtypeStruct((rows, d), F32), tail_shape, tail_shape],
        scratch_shapes=[pltpu.VMEM((tm, d), F32),
                        pltpu.VMEM((nj, SUBLANES, ft), F32), pltpu.VMEM((nj, SUBLANES, ft), F32),
                        pltpu.VMEM((2 * (ft // sub), n_streams, SUBLANES + stream_len, sub), F32)],
        compiler_params=pltpu.CompilerParams(
            dimension_semantics=("arbitrary", "arbitrary"), vmem_limit_bytes=_vmem_limit(est)),
        name="convffn",
    )(hn, x1, w_up, w_up, conv_w, conv_w, conv_b, conv_b, hist, hist, w_down, g_post)
    return y, jnp.concatenate([cg[-1], cv[-1]], axis=-1)


def _layer(x, pos, hist, attend, lw, *, gmlp_len, transposed, emit_vn, n_streams, tm):
    (g_mix_pre, w_in, gv, w_s, b_s, subln_g, w_out, g_mix_post, g_ffn_pre, w_up, conv_w, conv_b,
     w_down, g_ffn_post) = lw
    res = _inproj(x, pos, g_mix_pre, w_in, gv, w_s, b_s, gmlp_len=gmlp_len, transposed=transposed,
                  emit_vn=emit_vn, tm=tm)
    out_a, q, k32, kbf, v32, vbf = res[:6]
    vn = res[6] if emit_vn else None
    out_b = attend(q, kbf, vbf)
    x1, hn2 = _outproj(x, out_a, out_b, w_out, g_mix_post, g_ffn_pre, tm=tm)
    y, new_hist = _ffn(hn2, x1, w_up, conv_w, conv_b, hist, w_down, g_ffn_post,
                       tm=tm, ft=FF_TILE, n_streams=n_streams)
    return y, k32, v32, vn, new_hist


def kernel(x_prompt, x_sample, cache_k, cache_v, state_conv, g_mix_pre, w_in, gmlp_v_gain, gmlp_w_s,
           gmlp_b_s, lambda_q1, lambda_k1, lambda_q2, lambda_k2, subln_g, w_out, g_mix_post,
           g_ffn_pre, w_up, conv_w, conv_b, w_down, g_ffn_post):
    b_p, s_p, d = x_prompt.shape
    b_s, t_new, _ = x_sample.shape
    depth = w_in.shape[0]
    past_len = cache_k.shape[2]
    d_ff = w_down.shape[1]
    assert b_p == 1
    pos_p = jnp.arange(s_p)
    pos_s = jnp.tile(past_len + jnp.arange(t_new), b_s)
    zero_hist = jnp.zeros((b_p, CONV_W - 1, 2 * d_ff), F32)
    row = lambda a: a.reshape(1, -1)

    yp = x_prompt.reshape(b_p * s_p, d)
    ys = x_sample.reshape(b_s * t_new, d)
    outs = [[] for _ in range(7)]
    for l in range(depth):
        lam_init = 0.8 - 0.6 * math.exp(-0.3 * l)
        lam_vecs = tuple(row(a[l]) for a in (lambda_q1, lambda_k1, lambda_q2, lambda_k2))
        sub_g = row(subln_g[l])
        lw = (row(g_mix_pre[l]), w_in[l].astype(BF16), row(gmlp_v_gain[l]), gmlp_w_s[l], gmlp_b_s[l],
              sub_g, w_out[l].astype(BF16), row(g_mix_post[l]), row(g_ffn_pre[l]),
              w_up[l].astype(BF16), conv_w[l], row(conv_b[l]), w_down[l].astype(BF16),
              row(g_ffn_post[l]))

        tq = min(ATT_Q_TILE, s_p)
        attend_p = lambda q, k, v: _prompt_attention(
            q, k, v, lam_vecs, sub_g, lam_init=lam_init, tq=tq, tk=min(ATT_K_TILE, tq))
        yp, kp, vp, _, cp = _layer(yp, pos_p, zero_hist, attend_p, lw, gmlp_len=GROUP_DIM,
                                   transposed=True, emit_vn=False, n_streams=1,
                                   tm=min(ROW_TILE, s_p))

        ck = cache_k[l].reshape(b_s, past_len, ATT_W)
        cv = cache_v[l].reshape(b_s, past_len, ATT_W)
        attend_s = lambda q, k, v: _sample_attention(
            q, k, v, ck, cv, lam_vecs, sub_g, lam_init=lam_init, t_new=t_new)
        ys, ks, vs, gs, cs = _layer(ys, pos_s, state_conv[l], attend_s, lw, gmlp_len=t_new,
                                    transposed=False, emit_vn=True, n_streams=b_s,
                                    tm=b_s * t_new)

        for acc, val in zip(outs, (
                kp.reshape(b_p, s_p, N_HEADS, 2, HEAD_DIM), vp.reshape(b_p, s_p, N_HEADS, HEAD_W), cp,
                ks.reshape(b_s, t_new, N_HEADS, 2, HEAD_DIM), vs.reshape(b_s, t_new, N_HEADS, HEAD_W),
                gs.reshape(b_s, t_new, N_GROUPS, GROUP_DIM), cs)):
            acc.append(val)

    return (yp.reshape(b_p, s_p, d), ys.reshape(b_s, t_new, d)) + tuple(jnp.stack(o) for o in outs)
```

```python
import functools
import math

import jax
import jax.numpy as jnp
from jax import lax
from jax.experimental import pallas as pl
from jax.experimental.pallas import tpu as pltpu

F32 = jnp.float32
BF16 = jnp.bfloat16

N_HEADS = 8
HEAD_DIM = 64
HEAD_W = 2 * HEAD_DIM
CHUNK = 64
ROT_DIM = HEAD_DIM // 4
ROT_HALF = ROT_DIM // 2
ROPE_THETA = 500000.0
ATT_SCALE = HEAD_DIM ** -0.5
N_GROUPS = 8
GROUP_DIM = 128
GMLP_W = N_GROUPS * GROUP_DIM
ATT_W = N_HEADS * HEAD_W
CONV_W = 3
EPS = 1e-6
LOG2E = 1.4426950408889634

LANES = 128
SUBLANES = 8
VMEM_BYTES_V7X = 64 * 1024 * 1024

ROW_TILE = 512
PROJ_ROW_TILE = 256
FF_TILE = 512
FF_SUBTILE = 256
ATT_Q_TILE = 512
ATT_K_TILE = 512


def _vmem_limit(estimate_bytes):
    return int(min(VMEM_BYTES_V7X - (4 << 20), max(32 << 20, estimate_bytes)))


def _gelu(x):
    a = -2.0 * math.sqrt(2.0 / math.pi) * LOG2E
    return x / (1.0 + jnp.exp2(x * (a + (a * 0.044715) * (x * x))))


def _rms(x, g):
    return x * lax.rsqrt(jnp.mean(x * x, axis=-1, keepdims=True) + EPS) * g


def _rope(z, cos, sin_lo, sin_hi):
    outs = []
    for s in range(z.shape[1] // LANES):
        xs = z[:, s * LANES:(s + 1) * LANES]
        nxt = pltpu.roll(xs, LANES - ROT_HALF, axis=1)
        prv = pltpu.roll(xs, ROT_HALF, axis=1)
        outs.append(xs * cos + nxt * sin_lo + prv * sin_hi)
    return jnp.concatenate(outs, axis=1)


def _inproj_kernel(x_ref, gpre_ref, w_ref, gv_ref, ws_ref, bs_ref, cos_ref, slo_ref, shi_ref,
                   *refs, gmlp_len, transposed, emit_vn, k_tile):
    if emit_vn:
        outa_ref, q_ref, k32_ref, kbf_ref, v32_ref, vbf_ref, vn_ref = refs
    else:
        outa_ref, q_ref, k32_ref, kbf_ref, v32_ref, vbf_ref = refs
        vn_ref = None
    tm = x_ref.shape[0]
    hn = _rms(x_ref[...], gpre_ref[...]).astype(BF16)
    block = lambda n: jnp.dot(hn, w_ref[:, n * GMLP_W:(n + 1) * GMLP_W],
                              preferred_element_type=F32)

    u = _gelu(block(0))
    vn = _rms(_gelu(block(1)), gv_ref[...])
    if emit_vn:
        vn_ref[...] = vn
    vnb = vn.astype(BF16)
    ln = gmlp_len
    causal = (lax.broadcasted_iota(jnp.int32, (ln, ln), 1)
              <= lax.broadcasted_iota(jnp.int32, (ln, ln), 0))
    for g in range(N_GROUPS):
        wg = jnp.where(causal, ws_ref[g], 0.0).astype(BF16)
        cols = slice(g * GROUP_DIM, (g + 1) * GROUP_DIM)
        bias = bs_ref[:, cols]
        for ch in range(tm // ln):
            rows = slice(ch * ln, (ch + 1) * ln)
            sg = jnp.dot(wg, vnb[rows, cols], preferred_element_type=F32) + bias
            outa_ref[rows, cols] = (u[rows, cols] * sg).astype(BF16)

    cos, slo, shi = cos_ref[...], slo_ref[...], shi_ref[...]
    q = _rope(block(2), cos, slo, shi) * (ATT_SCALE * LOG2E)
    q_ref[...] = q.T.astype(BF16) if transposed else q.astype(BF16)

    k = _rope(block(3), cos, slo, shi)
    k32_ref[...] = k
    kbf_ref[...] = k.astype(BF16)

    v = block(4)
    v32_ref[...] = v
    if not transposed:
        vbf_ref[...] = v.astype(BF16)
    else:
        vt = v.T.astype(BF16)
        if k_tile >= tm:
            vbf_ref[0] = vt
        else:
            for cb in range(tm // k_tile):
                vbf_ref[cb] = vt[:, cb * k_tile:(cb + 1) * k_tile]


def _rope_tables(start, count):
    lane = jnp.arange(LANES)
    in_block = lane % HEAD_DIM
    inv_freq = ROPE_THETA ** (-jnp.arange(ROT_HALF, dtype=F32) / ROT_HALF)
    freq = inv_freq[lane % ROT_HALF][None, :]
    a0 = start // LANES
    n_hi = -(-(start + count) // LANES) - a0
    hi = ((a0 + jnp.arange(n_hi)) * LANES).astype(F32)[:, None] * freq
    lo = jnp.arange(LANES).astype(F32)[:, None] * freq
    cos_hi, sin_hi = jnp.cos(hi)[:, None, :], jnp.sin(hi)[:, None, :]
    cos_lo, sin_lo = jnp.cos(lo)[None], jnp.sin(lo)[None]
    first = start - a0 * LANES
    rows = lambda t: t.reshape(n_hi * LANES, LANES)[first:first + count]
    cos = rows(cos_hi * cos_lo - sin_hi * sin_lo)
    sin = rows(sin_hi * cos_lo + cos_hi * sin_lo)
    rot_lo = in_block < ROT_HALF
    rot_hi = (in_block >= ROT_HALF) & (in_block < ROT_DIM)
    return (jnp.where(rot_lo | rot_hi, cos, 1.0), jnp.where(rot_lo, -sin, 0.0),
            jnp.where(rot_hi, sin, 0.0))


def _inproj(x, pos, g_pre, w_in, gv, w_s, b_s, *, gmlp_len, transposed, emit_vn, tm):
    rows, d = x.shape
    assert rows % tm == 0 and tm % gmlp_len == 0
    k_tile = min(ATT_K_TILE, rows)
    assert k_tile % tm == 0 or tm % k_tile == 0
    start, count, repeats = pos
    assert count * repeats == rows
    cos, slo, shi = (jnp.tile(t, (repeats, 1)) for t in _rope_tables(start, count))
    ws = w_s[:, :gmlp_len, :gmlp_len]
    bs = jnp.repeat(b_s[:, :gmlp_len].T, GROUP_DIM, axis=1)

    row_blk = lambda w: pl.BlockSpec((tm, w), lambda i: (i, 0))
    const = lambda shape: pl.BlockSpec(shape, lambda i: (0,) * len(shape), pipeline_mode=pl.Buffered(1))
    in_specs = [
        row_blk(d), const((1, d)), const(w_in.shape),
        const((1, GMLP_W)), const(ws.shape), const(bs.shape),
        row_blk(LANES), row_blk(LANES), row_blk(LANES),
    ]
    if transposed:
        q_shape, q_spec = (ATT_W, rows), pl.BlockSpec((ATT_W, tm), lambda i: (0, i))
        v_shape = (rows // k_tile, ATT_W, k_tile)
        if k_tile >= tm:
            per = k_tile // tm
            v_spec = pl.BlockSpec((1, ATT_W, tm), lambda i: (i // per, 0, i % per))
        else:
            v_spec = pl.BlockSpec((tm // k_tile, ATT_W, k_tile), lambda i: (i, 0, 0))
    else:
        q_shape, q_spec = (rows, ATT_W), row_blk(ATT_W)
        v_shape, v_spec = (rows, ATT_W), row_blk(ATT_W)
    out_shape = [
        jax.ShapeDtypeStruct((rows, GMLP_W), BF16), jax.ShapeDtypeStruct(q_shape, BF16),
        jax.ShapeDtypeStruct((rows, ATT_W), F32), jax.ShapeDtypeStruct((rows, ATT_W), BF16),
        jax.ShapeDtypeStruct((rows, ATT_W), F32), jax.ShapeDtypeStruct(v_shape, BF16),
    ]
    out_specs = [row_blk(GMLP_W), q_spec, row_blk(ATT_W), row_blk(ATT_W), row_blk(ATT_W), v_spec]
    if emit_vn:
        out_shape.append(jax.ShapeDtypeStruct((rows, GMLP_W), F32))
        out_specs.append(row_blk(GMLP_W))
    est = (2 * tm * d * 4 + w_in.size * 2 + tm * d * 2
           + 2 * tm * GMLP_W * (2 + 2 + 4 + 2 + 4 + 2 + 4) + 12 * tm * GMLP_W * 4)
    return pl.pallas_call(
        functools.partial(_inproj_kernel, gmlp_len=gmlp_len, transposed=transposed,
                          emit_vn=emit_vn, k_tile=k_tile),
        grid=(rows // tm,),
        in_specs=in_specs, out_specs=out_specs, out_shape=out_shape,
        compiler_params=pltpu.CompilerParams(
            dimension_semantics=("arbitrary",), vmem_limit_bytes=_vmem_limit(est)),
        name="inproj",
    )(x, g_pre, w_in, gv, ws, bs, cos, slo, shi)


def _lambda(lq1_ref, lk1_ref, lq2_ref, lk2_ref, lam_init):
    d1 = jnp.sum(lq1_ref[...] * lk1_ref[...], axis=-1, keepdims=True)
    d2 = jnp.sum(lq2_ref[...] * lk2_ref[...], axis=-1, keepdims=True)
    return jnp.exp(d1) - jnp.exp(d2) + lam_init


def _prompt_attn_kernel(qt_ref, k_ref, vt_ref, lq1_ref, lk1_ref, lq2_ref, lk2_ref, g_ref, out_ref,
                        s_buf, smax_buf, p_buf, alpha_buf, m_s, l_s, acc_s, *, tq, tk, lam_init):
    qi = pl.program_id(1)
    qt = qt_ref[...]
    comp = lax.broadcasted_iota(jnp.int32, qt.shape, 0) < HEAD_DIM
    zero = jnp.zeros_like(qt)
    qs = (jnp.where(comp, qt, zero), jnp.where(comp, zero, qt))

    m_s[...] = jnp.full(m_s.shape, -jnp.inf, F32)
    l_s[...] = jnp.zeros(l_s.shape, F32)
    acc_s[...] = jnp.zeros(acc_s.shape, F32)

    steps_per_tile = tq // tk
    n_full = qi * steps_per_tile
    n_total = n_full + steps_per_tile

    def block_of(j):
        if isinstance(j, int):
            return n_full + j if j < steps_per_tile else j - steps_per_tile
        return jnp.where(j < steps_per_tile, n_full + j, j - steps_per_tile)

    def scores(j):
        kb = block_of(j)
        kblk = k_ref[pl.ds(pl.multiple_of(kb * tk, tk), tk), :]
        masked = isinstance(j, int) and j < steps_per_tile
        if masked:
            key_chunk = (lax.broadcasted_iota(jnp.int32, (tk, tq), 0) + j * tk) // CHUNK
            allowed = key_chunk <= lax.broadcasted_iota(jnp.int32, (tk, tq), 1) // CHUNK
        for c in range(2):
            s = jnp.dot(kblk, qs[c], preferred_element_type=F32)
            if masked:
                s = jnp.where(allowed, s, -jnp.inf)
            s_buf[c] = s
            smax_buf[c] = jnp.max(s, axis=0, keepdims=True)

    def softmax():
        for c in range(2):
            m_old = m_s[c]
            m_new = jnp.maximum(m_old, smax_buf[c])
            alpha = jnp.exp2(m_old - m_new)
            alpha_buf[c] = alpha
            p = jnp.exp2(s_buf[c] - m_new)
            l_s[c] = alpha * l_s[c] + jnp.sum(p, axis=0, keepdims=True)
            p_buf[c] = p.astype(BF16)
            m_s[c] = m_new

    def values(j):
        vtb = vt_ref[block_of(j)]
        for c in range(2):
            acc_s[c] = alpha_buf[c] * acc_s[c] + jnp.dot(vtb, p_buf[c], preferred_element_type=F32)

    def step(u, with_values=True, with_softmax=True):
        if with_values:
            values(u - 2)
        if with_softmax:
            softmax()
        scores(u)

    def body(u, carry):
        step(u)
        return carry

    def drain():
        values(n_total - 2)
        softmax()

    if steps_per_tile >= 2:
        for u in range(steps_per_tile):
            step(u, with_values=u >= 2, with_softmax=u >= 1)
        lax.fori_loop(steps_per_tile, n_total, body, 0)
        drain()
    else:
        step(0, with_values=False, with_softmax=False)

        @pl.when(qi > 0)
        def _():
            step(1, with_values=False)
            lax.fori_loop(2, n_total, body, 0)
            drain()

        pl.when(qi == 0)(softmax)
    values(n_total - 1)

    lam = _lambda(lq1_ref, lk1_ref, lq2_ref, lk2_ref, lam_init)
    o = acc_s[0] / l_s[0] - lam * (acc_s[1] / l_s[1])
    on = o * lax.rsqrt(jnp.mean(o * o, axis=0, keepdims=True) + EPS)
    out_ref[...] = (on.T * (g_ref[...] * (1.0 - lam_init))).astype(BF16)


def _prompt_attention(qt, kbf, vt, lam_vecs, subln_g, *, lam_init, tq, tk):
    rows = kbf.shape[0]
    assert rows % tq == 0 and tq % tk == 0 and tk % CHUNK == 0 and vt.shape[2] == tk
    vec = pl.BlockSpec((1, HEAD_DIM), lambda h, i: (0, 0))
    est = 2 * 2 * rows * HEAD_W * 2 + 2 * HEAD_W * tq * 4 + 12 * tk * tq * 4
    return pl.pallas_call(
        functools.partial(_prompt_attn_kernel, tq=tq, tk=tk, lam_init=lam_init),
        grid=(N_HEADS, rows // tq),
        in_specs=[
            pl.BlockSpec((HEAD_W, tq), lambda h, i: (h, i)),
            pl.BlockSpec((rows, HEAD_W), lambda h, i: (0, h)),
            pl.BlockSpec((rows // tk, HEAD_W, tk), lambda h, i: (0, h, 0)),
            vec, vec, vec, vec,
            pl.BlockSpec((1, HEAD_W), lambda h, i: (0, 0)),
        ],
        out_specs=pl.BlockSpec((tq, HEAD_W), lambda h, i: (i, h)),
        out_shape=jax.ShapeDtypeStruct((rows, ATT_W), BF16),
        scratch_shapes=[pltpu.VMEM((2, tk, tq), F32), pltpu.VMEM((2, 1, tq), F32),
                        pltpu.VMEM((2, tk, tq), BF16), pltpu.VMEM((2, 1, tq), F32),
                        pltpu.VMEM((2, 1, tq), F32), pltpu.VMEM((2, 1, tq), F32),
                        pltpu.VMEM((2, HEAD_W, tq), F32)],
        compiler_params=pltpu.CompilerParams(
            dimension_semantics=("arbitrary", "arbitrary"), vmem_limit_bytes=_vmem_limit(est)),
        name="prompt_attn",
    )(qt, kbf, vt, *lam_vecs, subln_g)


def _sample_attn_kernel(q_ref, kc_ref, vc_ref, kn_ref, vn_ref, lq1_ref, lk1_ref, lq2_ref, lk2_ref,
                        g_ref, out_ref, *, past_len, lam_init):
    q = q_ref[...]
    t = q.shape[0]
    comp = lax.broadcasted_iota(jnp.int32, q.shape, 1) < HEAD_DIM
    zero = jnp.zeros_like(q)
    qs = (jnp.where(comp, q, zero), jnp.where(comp, zero, q))
    kc = kc_ref[0]
    vc = vc_ref[0]
    kn = kn_ref[...]
    vn = vn_ref[...]
    q_chunk = (past_len + lax.broadcasted_iota(jnp.int32, (t, 1), 0)) // CHUNK
    ok_c = lax.broadcasted_iota(jnp.int32, (t, past_len), 1) // CHUNK <= q_chunk
    ok_n = (past_len + lax.broadcasted_iota(jnp.int32, (t, t), 1)) // CHUNK <= q_chunk
    nt = (((1,), (1,)), ((), ()))
    outs = []
    for c in range(2):
        sc = lax.dot_general(qs[c], kc, nt, preferred_element_type=F32)
        sn = lax.dot_general(qs[c], kn, nt, preferred_element_type=F32)
        sc = jnp.where(ok_c, sc, -jnp.inf)
        sn = jnp.where(ok_n, sn, -jnp.inf)
        m = jnp.maximum(jnp.max(sc, axis=-1, keepdims=True), jnp.max(sn, axis=-1, keepdims=True))
        pc = jnp.exp2(sc - m)
        pn = jnp.exp2(sn - m)
        den = jnp.sum(pc, axis=-1, keepdims=True) + jnp.sum(pn, axis=-1, keepdims=True)
        num = (jnp.dot(pc.astype(BF16), vc, preferred_element_type=F32)
               + jnp.dot(pn.astype(BF16), vn, preferred_element_type=F32))
        outs.append(num / den)
    lam = _lambda(lq1_ref, lk1_ref, lq2_ref, lk2_ref, lam_init)
    o = outs[0] - lam * outs[1]
    out_ref[...] = (_rms(o, g_ref[...]) * (1.0 - lam_init)).astype(BF16)


def _sample_attention(q, kbf, vbf, cache_k, cache_v, lam_vecs, subln_g, *, lam_init, t_new):
    n_streams, past_len = cache_k.shape[:2]
    vec = pl.BlockSpec((1, HEAD_DIM), lambda b, h: (0, 0))
    new_blk = pl.BlockSpec((t_new, HEAD_W), lambda b, h: (b, h))
    cache_blk = pl.BlockSpec((1, past_len, HEAD_W), lambda b, h: (b, 0, h))
    return pl.pallas_call(
        functools.partial(_sample_attn_kernel, past_len=past_len, lam_init=lam_init),
        grid=(n_streams, N_HEADS),
        in_specs=[new_blk, cache_blk, cache_blk, new_blk, new_blk, vec, vec, vec, vec,
                  pl.BlockSpec((1, HEAD_W), lambda b, h: (0, 0))],
        out_specs=new_blk,
        out_shape=jax.ShapeDtypeStruct((n_streams * t_new, ATT_W), BF16),
        compiler_params=pltpu.CompilerParams(dimension_semantics=("arbitrary", "arbitrary")),
        name="sample_attn",
    )(q, cache_k, cache_v, kbf, vbf, *lam_vecs, subln_g)


def _outproj_kernel(x_ref, a_ref, b_ref, w_ref, gpost_ref, gffn_ref, x1_ref, hn_ref):
    mix = (jnp.dot(a_ref[...], w_ref[:GMLP_W, :], preferred_element_type=F32)
           + jnp.dot(b_ref[...], w_ref[GMLP_W:, :], preferred_element_type=F32))
    x1 = x_ref[...] + _rms(mix, gpost_ref[...])
    x1_ref[...] = x1
    hn_ref[...] = _rms(x1, gffn_ref[...]).astype(BF16)


def _outproj(x, out_a, out_b, w_out, g_post, g_ffn, *, tm):
    rows, d = x.shape
    row_blk = lambda w: pl.BlockSpec((tm, w), lambda i: (i, 0))
    const = lambda shape: pl.BlockSpec(shape, lambda i: (0,) * len(shape))
    est = 2 * tm * d * (4 + 4 + 2) + 2 * 2 * tm * GMLP_W * 2 + 2 * w_out.size * 2 + 4 * tm * d * 4
    return pl.pallas_call(
        _outproj_kernel,
        grid=(rows // tm,),
        in_specs=[row_blk(d), row_blk(GMLP_W), row_blk(ATT_W), const(w_out.shape),
                  const((1, d)), const((1, d))],
        out_specs=[row_blk(d), row_blk(d)],
        out_shape=[jax.ShapeDtypeStruct((rows, d), F32), jax.ShapeDtypeStruct((rows, d), BF16)],
        compiler_params=pltpu.CompilerParams(
            dimension_semantics=("arbitrary",), vmem_limit_bytes=_vmem_limit(est)),
        name="outproj",
    )(x, out_a, out_b, w_out, g_post, g_ffn)


def _ffn_kernel(hn_ref, x1_ref, wg_ref, wv_ref, cwg_ref, cwv_ref, cbg_ref, cbv_ref, hg_ref, hv_ref,
                wd_ref, gpost_ref, y_ref, cg_ref, cv_ref, acc_s, carg_s, carv_s, stage_s,
                *, n_streams, stream_len):
    i = pl.program_id(0)
    j = pl.program_id(1)
    tm = hn_ref.shape[0]
    ft = wg_ref.shape[1]
    hn = hn_ref[...]
    sub = min(FF_SUBTILE, ft)

    @pl.when(j == 0)
    def _():
        acc_s[...] = jnp.zeros(acc_s.shape, F32)

    if n_streams == 1:
        @pl.when(i == 0)
        def _():
            pad = jnp.zeros((SUBLANES - (CONV_W - 1), ft), F32)
            carg_s[j] = jnp.concatenate([pad, hg_ref[0]], axis=0)
            carv_s[j] = jnp.concatenate([pad, hv_ref[0]], axis=0)

    def conv_branch(buf, cs, w_ref, cw_ref, cb_ref, h_ref, car_s, cout_ref):
        up = jnp.dot(hn, w_ref[:, cs], preferred_element_type=F32)
        taps = [[], [], []]
        for s in range(n_streams):
            lo = s * stream_len
            if n_streams == 1:
                stage_s[buf, s, 0:SUBLANES] = car_s[j][:, cs]
            else:
                stage_s[buf, s, SUBLANES - (CONV_W - 1):SUBLANES] = h_ref[s, :, cs]
            stage_s[buf, s, SUBLANES:SUBLANES + stream_len] = up[lo:lo + stream_len]
            for k in range(CONV_W):
                first = SUBLANES - (CONV_W - 1) + k
                taps[k].append(stage_s[buf, s, first:first + stream_len])
            cout_ref[0, s, :, cs] = stage_s[buf, s, SUBLANES + stream_len - (CONV_W - 1):
                                            SUBLANES + stream_len]
        if n_streams == 1:
            car_s[j, :, cs] = stage_s[buf, 0, stream_len:stream_len + SUBLANES]
        taps = [t[0] if n_streams == 1 else jnp.concatenate(t, axis=0) for t in taps]
        cw = cw_ref[:, cs]
        return taps[0] * cw[0:1] + taps[1] * cw[1:2] + taps[2] * cw[2:3] + cb_ref[:, cs]

    part = None
    for c0 in range(0, ft, sub):
        cs = slice(c0, c0 + sub)
        buf = 2 * (c0 // sub)
        gate = conv_branch(buf, cs, wg_ref, cwg_ref, cbg_ref, hg_ref, carg_s, cg_ref)
        val = conv_branch(buf + 1, cs, wv_ref, cwv_ref, cbv_ref, hv_ref, carv_s, cv_ref)
        act = (_gelu(gate) * val).astype(BF16)
        d_part = jnp.dot(act, wd_ref[cs, :], preferred_element_type=F32)
        part = d_part if part is None else part + d_part
    acc_s[...] += part

    @pl.when(j == pl.num_programs(1) - 1)
    def _():
        y_ref[...] = x1_ref[...] + _rms(acc_s[...], gpost_ref[...])


def _ffn(hn, x1, w_up, conv_w, conv_b, hist, w_down, g_post, *, tm, ft, n_streams):
    rows, d = x1.shape
    d_ff = w_down.shape[0]
    nj = d_ff // ft
    n_tiles = rows // tm
    stream_len = tm // n_streams
    sub = min(FF_SUBTILE, ft)
    assert rows % tm == 0 and d_ff % ft == 0 and (n_streams == 1 or n_tiles == 1)
    assert hist.shape == (n_streams, CONV_W - 1, 2 * d_ff) and stream_len % SUBLANES == 0
    gate_col = lambda i, j: (0, j)
    val_col = lambda i, j: (0, nj + j)
    row_blk = lambda w: pl.BlockSpec((tm, w), lambda i, j: (i, 0))
    once_per_tile = lambda w: pl.BlockSpec((tm, w), lambda i, j: (i, 0), pipeline_mode=pl.Buffered(1))
    hist_blk = lambda off: pl.BlockSpec((n_streams, CONV_W - 1, ft), lambda i, j: (0, 0, off + j))
    tail_blk = pl.BlockSpec((1, n_streams, CONV_W - 1, ft), lambda i, j: (i, 0, 0, j))
    tail_shape = jax.ShapeDtypeStruct((n_tiles, n_streams, CONV_W - 1, d_ff), F32)
    est = (tm * d * (2 + 4 + 2 * 4) + tm * d * 4 + 2 * 3 * d * ft * 2 + 16 * tm * FF_SUBTILE * 4
           + 2 * tm * d * 4)
    y, cg, cv = pl.pallas_call(
        functools.partial(_ffn_kernel, n_streams=n_streams, stream_len=stream_len),
        grid=(n_tiles, nj),
        in_specs=[
            once_per_tile(d), once_per_tile(d),
            pl.BlockSpec((d, ft), gate_col), pl.BlockSpec((d, ft), val_col),
            pl.BlockSpec((CONV_W, ft), gate_col), pl.BlockSpec((CONV_W, ft), val_col),
            pl.BlockSpec((1, ft), gate_col), pl.BlockSpec((1, ft), val_col),
            hist_blk(0), hist_blk(nj),
            pl.BlockSpec((ft, d), lambda i, j: (j, 0)),
            pl.BlockSpec((1, d), lambda i, j: (0, 0)),
        ],
        out_specs=[row_blk(d), tail_blk, tail_blk],
        out_shape=[jax.ShapeDtypeStruct((rows, d), F32), tail_shape, tail_shape],
        scratch_shapes=[pltpu.VMEM((tm, d), F32),
                        pltpu.VMEM((nj, SUBLANES, ft), F32), pltpu.VMEM((nj, SUBLANES, ft), F32),
                        pltpu.VMEM((2 * (ft // sub), n_streams, SUBLANES + stream_len, sub), F32)],
        compiler_params=pltpu.CompilerParams(
            dimension_semantics=("arbitrary", "arbitrary"), vmem_limit_bytes=_vmem_limit(est)),
        name="convffn",
    )(hn, x1, w_up, w_up, conv_w, conv_w, conv_b, conv_b, hist, hist, w_down, g_post)
    return y, jnp.concatenate([cg[-1], cv[-1]], axis=-1)


def _layer(x, pos, hist, attend, lw, *, gmlp_len, transposed, emit_vn, n_streams, tm):
    (g_mix_pre, w_in, gv, w_s, b_s, subln_g, w_out, g_mix_post, g_ffn_pre, w_up, conv_w, conv_b,
     w_down, g_ffn_post) = lw
    res = _inproj(x, pos, g_mix_pre, w_in, gv, w_s, b_s, gmlp_len=gmlp_len, transposed=transposed,
                  emit_vn=emit_vn, tm=min(PROJ_ROW_TILE, tm))
    out_a, q, k32, kbf, v32, vbf = res[:6]
    vn = res[6] if emit_vn else None
    out_b = attend(q, kbf, vbf)
    x1, hn2 = _outproj(x, out_a, out_b, w_out, g_mix_post, g_ffn_pre, tm=tm)
    y, new_hist = _ffn(hn2, x1, w_up, conv_w, conv_b, hist, w_down, g_ffn_post,
                       tm=tm, ft=FF_TILE, n_streams=n_streams)
    return y, k32, v32, vn, new_hist


def kernel(x_prompt, x_sample, cache_k, cache_v, state_conv, g_mix_pre, w_in, gmlp_v_gain, gmlp_w_s,
           gmlp_b_s, lambda_q1, lambda_k1, lambda_q2, lambda_k2, subln_g, w_out, g_mix_post,
           g_ffn_pre, w_up, conv_w, conv_b, w_down, g_ffn_post):
    b_p, s_p, d = x_prompt.shape
    b_s, t_new, _ = x_sample.shape
    depth = w_in.shape[0]
    past_len = cache_k.shape[2]
    d_ff = w_down.shape[1]
    assert b_p == 1
    pos_p = (0, s_p, 1)
    pos_s = (past_len, t_new, b_s)
    zero_hist = jnp.zeros((b_p, CONV_W - 1, 2 * d_ff), F32)
    row = lambda a: a.reshape(1, -1)

    yp = x_prompt.reshape(b_p * s_p, d)
    ys = x_sample.reshape(b_s * t_new, d)
    outs = [[] for _ in range(7)]
    for l in range(depth):
        lam_init = 0.8 - 0.6 * math.exp(-0.3 * l)
        lam_vecs = tuple(row(a[l]) for a in (lambda_q1, lambda_k1, lambda_q2, lambda_k2))
        sub_g = row(subln_g[l])
        lw = (row(g_mix_pre[l]), w_in[l].astype(BF16), row(gmlp_v_gain[l]), gmlp_w_s[l], gmlp_b_s[l],
              sub_g, w_out[l].astype(BF16), row(g_mix_post[l]), row(g_ffn_pre[l]),
              w_up[l].astype(BF16), conv_w[l], row(conv_b[l]), w_down[l].astype(BF16),
              row(g_ffn_post[l]))

        tq = min(ATT_Q_TILE, s_p)
        attend_p = lambda q, k, v: _prompt_attention(
            q, k, v, lam_vecs, sub_g, lam_init=lam_init, tq=tq, tk=min(ATT_K_TILE, tq))
        yp, kp, vp, _, cp = _layer(yp, pos_p, zero_hist, attend_p, lw, gmlp_len=GROUP_DIM,
                                   transposed=True, emit_vn=False, n_streams=1,
                                   tm=min(ROW_TILE, s_p))

        ck = cache_k[l].reshape(b_s, past_len, ATT_W).astype(BF16)
        cv = cache_v[l].reshape(b_s, past_len, ATT_W).astype(BF16)
        attend_s = lambda q, k, v: _sample_attention(
            q, k, v, ck, cv, lam_vecs, sub_g, lam_init=lam_init, t_new=t_new)
        ys, ks, vs, gs, cs = _layer(ys, pos_s, state_conv[l], attend_s, lw, gmlp_len=t_new,
                                    transposed=False, emit_vn=True, n_streams=b_s,
                                    tm=b_s * t_new)

        for acc, val in zip(outs, (
                kp.reshape(b_p, s_p, N_HEADS, 2, HEAD_DIM), vp.reshape(b_p, s_p, N_HEADS, HEAD_W), cp,
                ks.reshape(b_s, t_new, N_HEADS, 2, HEAD_DIM), vs.reshape(b_s, t_new, N_HEADS, HEAD_W),
                gs.reshape(b_s, t_new, N_GROUPS, GROUP_DIM), cs)):
            acc.append(val)

    return (yp.reshape(b_p, s_p, d), ys.reshape(b_s, t_new, d)) + tuple(jnp.stack(o) for o in outs)
```

```python
import functools
import math

import jax
import jax.numpy as jnp
from jax import lax
from jax.experimental import pallas as pl
from jax.experimental.pallas import tpu as pltpu

F32 = jnp.float32
BF16 = jnp.bfloat16

N_HEADS = 8
HEAD_DIM = 64
HEAD_W = 2 * HEAD_DIM
CHUNK = 64
ROT_DIM = HEAD_DIM // 4
ROT_HALF = ROT_DIM // 2
ROPE_THETA = 500000.0
ATT_SCALE = HEAD_DIM ** -0.5
N_GROUPS = 8
GROUP_DIM = 128
GMLP_W = N_GROUPS * GROUP_DIM
ATT_W = N_HEADS * HEAD_W
CONV_W = 3
EPS = 1e-6
LOG2E = 1.4426950408889634

LANES = 128
SUBLANES = 8
VMEM_BYTES_V7X = 64 * 1024 * 1024

ROW_TILE = 512
PROJ_ROW_TILE = 256
FF_TILE = 512
FF_SUBTILE = 256
ATT_Q_TILE = 512
ATT_K_TILE = 512


def _vmem_limit(estimate_bytes):
    return int(min(VMEM_BYTES_V7X - (4 << 20), max(32 << 20, estimate_bytes)))


def _gelu(x):
    a = -2.0 * math.sqrt(2.0 / math.pi) * LOG2E
    return x / (1.0 + jnp.exp2(x * (a + (a * 0.044715) * (x * x))))


def _rms(x, g):
    return x * lax.rsqrt(jnp.mean(x * x, axis=-1, keepdims=True) + EPS) * g


def _rope(z, cos, sin_lo, sin_hi):
    outs = []
    for s in range(z.shape[1] // LANES):
        xs = z[:, s * LANES:(s + 1) * LANES]
        nxt = pltpu.roll(xs, LANES - ROT_HALF, axis=1)
        prv = pltpu.roll(xs, ROT_HALF, axis=1)
        outs.append(xs * cos + nxt * sin_lo + prv * sin_hi)
    return jnp.concatenate(outs, axis=1)


def _inproj_kernel(x_ref, gpre_ref, w_ref, gv_ref, ws_ref, bs_ref, cos_ref, slo_ref, shi_ref,
                   *refs, gmlp_len, transposed, emit_vn, k_tile):
    if emit_vn:
        outa_ref, q_ref, k32_ref, kbf_ref, v32_ref, vbf_ref, vn_ref = refs
    else:
        outa_ref, q_ref, k32_ref, kbf_ref, v32_ref, vbf_ref = refs
        vn_ref = None
    tm = x_ref.shape[0]
    hn = _rms(x_ref[...], gpre_ref[...]).astype(BF16)
    block = lambda n: jnp.dot(hn, w_ref[:, n * GMLP_W:(n + 1) * GMLP_W],
                              preferred_element_type=F32)

    u = _gelu(block(0))
    vn = _rms(_gelu(block(1)), gv_ref[...])
    if emit_vn:
        vn_ref[...] = vn
    vnb = vn.astype(BF16)
    ln = gmlp_len
    causal = (lax.broadcasted_iota(jnp.int32, (ln, ln), 1)
              <= lax.broadcasted_iota(jnp.int32, (ln, ln), 0))
    for g in range(N_GROUPS):
        wg = jnp.where(causal, ws_ref[g], 0.0).astype(BF16)
        cols = slice(g * GROUP_DIM, (g + 1) * GROUP_DIM)
        bias = bs_ref[:, cols]
        for ch in range(tm // ln):
            rows = slice(ch * ln, (ch + 1) * ln)
            sg = jnp.dot(wg, vnb[rows, cols], preferred_element_type=F32) + bias
            outa_ref[rows, cols] = (u[rows, cols] * sg).astype(BF16)

    cos, slo, shi = cos_ref[...], slo_ref[...], shi_ref[...]
    q = _rope(block(2), cos, slo, shi) * (ATT_SCALE * LOG2E)
    q_ref[...] = q.T.astype(BF16) if transposed else q.astype(BF16)

    k = _rope(block(3), cos, slo, shi)
    k32_ref[...] = k
    kbf_ref[...] = k.astype(BF16)

    v = block(4)
    v32_ref[...] = v
    if not transposed:
        vbf_ref[...] = v.astype(BF16)
    else:
        vt = v.T.astype(BF16)
        if k_tile >= tm:
            vbf_ref[0] = vt
        else:
            for cb in range(tm // k_tile):
                vbf_ref[cb] = vt[:, cb * k_tile:(cb + 1) * k_tile]


def _rope_tables(start, count):
    lane = jnp.arange(LANES)
    in_block = lane % HEAD_DIM
    inv_freq = ROPE_THETA ** (-jnp.arange(ROT_HALF, dtype=F32) / ROT_HALF)
    freq = inv_freq[lane % ROT_HALF][None, :]
    a0 = start // LANES
    n_hi = -(-(start + count) // LANES) - a0
    hi = ((a0 + jnp.arange(n_hi)) * LANES).astype(F32)[:, None] * freq
    lo = jnp.arange(LANES).astype(F32)[:, None] * freq
    cos_hi, sin_hi = jnp.cos(hi)[:, None, :], jnp.sin(hi)[:, None, :]
    cos_lo, sin_lo = jnp.cos(lo)[None], jnp.sin(lo)[None]
    first = start - a0 * LANES
    rows = lambda t: t.reshape(n_hi * LANES, LANES)[first:first + count]
    cos = rows(cos_hi * cos_lo - sin_hi * sin_lo)
    sin = rows(sin_hi * cos_lo + cos_hi * sin_lo)
    rot_lo = in_block < ROT_HALF
    rot_hi = (in_block >= ROT_HALF) & (in_block < ROT_DIM)
    return (jnp.where(rot_lo | rot_hi, cos, 1.0), jnp.where(rot_lo, -sin, 0.0),
            jnp.where(rot_hi, sin, 0.0))


def _inproj(x, pos, g_pre, w_in, gv, w_s, b_s, *, gmlp_len, transposed, emit_vn, tm):
    rows, d = x.shape
    assert rows % tm == 0 and tm % gmlp_len == 0
    k_tile = min(ATT_K_TILE, rows)
    assert k_tile % tm == 0 or tm % k_tile == 0
    start, count, repeats = pos
    assert count * repeats == rows
    cos, slo, shi = (jnp.tile(t, (repeats, 1)) for t in _rope_tables(start, count))
    ws = w_s[:, :gmlp_len, :gmlp_len]
    bs = jnp.repeat(b_s[:, :gmlp_len].T, GROUP_DIM, axis=1)

    row_blk = lambda w: pl.BlockSpec((tm, w), lambda i: (i, 0))
    const = lambda shape: pl.BlockSpec(shape, lambda i: (0,) * len(shape), pipeline_mode=pl.Buffered(1))
    in_specs = [
        row_blk(d), const((1, d)), const(w_in.shape),
        const((1, GMLP_W)), const(ws.shape), const(bs.shape),
        row_blk(LANES), row_blk(LANES), row_blk(LANES),
    ]
    if transposed:
        q_shape, q_spec = (ATT_W, rows), pl.BlockSpec((ATT_W, tm), lambda i: (0, i))
        v_shape = (rows // k_tile, ATT_W, k_tile)
        if k_tile >= tm:
            per = k_tile // tm
            v_spec = pl.BlockSpec((1, ATT_W, tm), lambda i: (i // per, 0, i % per))
        else:
            v_spec = pl.BlockSpec((tm // k_tile, ATT_W, k_tile), lambda i: (i, 0, 0))
    else:
        q_shape, q_spec = (rows, ATT_W), row_blk(ATT_W)
        v_shape, v_spec = (rows, ATT_W), row_blk(ATT_W)
    out_shape = [
        jax.ShapeDtypeStruct((rows, GMLP_W), BF16), jax.ShapeDtypeStruct(q_shape, BF16),
        jax.ShapeDtypeStruct((rows, ATT_W), F32), jax.ShapeDtypeStruct((rows, ATT_W), BF16),
        jax.ShapeDtypeStruct((rows, ATT_W), F32), jax.ShapeDtypeStruct(v_shape, BF16),
    ]
    out_specs = [row_blk(GMLP_W), q_spec, row_blk(ATT_W), row_blk(ATT_W), row_blk(ATT_W), v_spec]
    if emit_vn:
        out_shape.append(jax.ShapeDtypeStruct((rows, GMLP_W), F32))
        out_specs.append(row_blk(GMLP_W))
    est = (2 * tm * d * 4 + w_in.size * 2 + tm * d * 2
           + 2 * tm * GMLP_W * (2 + 2 + 4 + 2 + 4 + 2 + 4) + 12 * tm * GMLP_W * 4)
    return pl.pallas_call(
        functools.partial(_inproj_kernel, gmlp_len=gmlp_len, transposed=transposed,
                          emit_vn=emit_vn, k_tile=k_tile),
        grid=(rows // tm,),
        in_specs=in_specs, out_specs=out_specs, out_shape=out_shape,
        compiler_params=pltpu.CompilerParams(
            dimension_semantics=("arbitrary",), vmem_limit_bytes=_vmem_limit(est)),
        name="inproj",
    )(x, g_pre, w_in, gv, ws, bs, cos, slo, shi)


def _lambda(lq1_ref, lk1_ref, lq2_ref, lk2_ref, lam_init):
    d1 = jnp.sum(lq1_ref[...] * lk1_ref[...], axis=-1, keepdims=True)
    d2 = jnp.sum(lq2_ref[...] * lk2_ref[...], axis=-1, keepdims=True)
    return jnp.exp(d1) - jnp.exp(d2) + lam_init


def _prompt_attn_kernel(qt_ref, k_ref, vt_ref, lq1_ref, lk1_ref, lq2_ref, lk2_ref, g_ref, out_ref,
                        s_buf, smax_buf, p_buf, alpha_buf, m_s, l_s, acc_s, *, tq, tk, lam_init):
    qi = pl.program_id(1)
    qt = qt_ref[...]
    comp = lax.broadcasted_iota(jnp.int32, qt.shape, 0) < HEAD_DIM
    zero = jnp.zeros_like(qt)
    qs = (jnp.where(comp, qt, zero), jnp.where(comp, zero, qt))

    m_s[...] = jnp.full(m_s.shape, -jnp.inf, F32)
    l_s[...] = jnp.zeros(l_s.shape, F32)
    acc_s[...] = jnp.zeros(acc_s.shape, F32)

    steps_per_tile = tq // tk
    n_full = qi * steps_per_tile
    n_total = n_full + steps_per_tile

    def block_of(j):
        if isinstance(j, int):
            return n_full + j if j < steps_per_tile else j - steps_per_tile
        return jnp.where(j < steps_per_tile, n_full + j, j - steps_per_tile)

    def scores(j):
        kb = block_of(j)
        kblk = k_ref[pl.ds(pl.multiple_of(kb * tk, tk), tk), :]
        masked = isinstance(j, int) and j < steps_per_tile
        if masked:
            key_chunk = (lax.broadcasted_iota(jnp.int32, (tk, tq), 0) + j * tk) // CHUNK
            allowed = key_chunk <= lax.broadcasted_iota(jnp.int32, (tk, tq), 1) // CHUNK
        for c in range(2):
            s = jnp.dot(kblk, qs[c], preferred_element_type=F32)
            if masked:
                s = jnp.where(allowed, s, -jnp.inf)
            s_buf[c] = s
            smax_buf[c] = jnp.max(s, axis=0, keepdims=True)

    def softmax():
        for c in range(2):
            m_old = m_s[c]
            m_new = jnp.maximum(m_old, smax_buf[c])
            alpha = jnp.exp2(m_old - m_new)
            alpha_buf[c] = alpha
            p = jnp.exp2(s_buf[c] - m_new)
            l_s[c] = alpha * l_s[c] + jnp.sum(p, axis=0, keepdims=True)
            p_buf[c] = p.astype(BF16)
            m_s[c] = m_new

    def values(j):
        vtb = vt_ref[block_of(j)]
        for c in range(2):
            acc_s[c] = alpha_buf[c] * acc_s[c] + jnp.dot(vtb, p_buf[c], preferred_element_type=F32)

    def step(u, with_values=True, with_softmax=True):
        if with_values:
            values(u - 2)
        if with_softmax:
            softmax()
        scores(u)

    def run_steps(first, stop):
        count = stop - first

        def pair(v, carry):
            step(first + 2 * v)
            step(first + 2 * v + 1)
            return carry

        lax.fori_loop(0, count // 2, pair, 0)
        pl.when(count % 2 == 1)(lambda: step(stop - 1))

    def drain():
        values(n_total - 2)
        softmax()

    if steps_per_tile >= 2:
        for u in range(steps_per_tile):
            step(u, with_values=u >= 2, with_softmax=u >= 1)
        run_steps(steps_per_tile, n_total)
        drain()
    else:
        step(0, with_values=False, with_softmax=False)

        @pl.when(qi > 0)
        def _():
            step(1, with_values=False)
            run_steps(2, n_total)
            drain()

        pl.when(qi == 0)(softmax)
    values(n_total - 1)

    lam = _lambda(lq1_ref, lk1_ref, lq2_ref, lk2_ref, lam_init)
    o = acc_s[0] / l_s[0] - lam * (acc_s[1] / l_s[1])
    on = o * lax.rsqrt(jnp.mean(o * o, axis=0, keepdims=True) + EPS)
    out_ref[...] = (on.T * (g_ref[...] * (1.0 - lam_init))).astype(BF16)


def _prompt_attention(qt, kbf, vt, lam_vecs, subln_g, *, lam_init, tq, tk):
    rows = kbf.shape[0]
    assert rows % tq == 0 and tq % tk == 0 and tk % CHUNK == 0 and vt.shape[2] == tk
    vec = pl.BlockSpec((1, HEAD_DIM), lambda h, i: (0, 0))
    est = 2 * 2 * rows * HEAD_W * 2 + 2 * HEAD_W * tq * 4 + 12 * tk * tq * 4
    return pl.pallas_call(
        functools.partial(_prompt_attn_kernel, tq=tq, tk=tk, lam_init=lam_init),
        grid=(N_HEADS, rows // tq),
        in_specs=[
            pl.BlockSpec((HEAD_W, tq), lambda h, i: (h, i)),
            pl.BlockSpec((rows, HEAD_W), lambda h, i: (0, h)),
            pl.BlockSpec((rows // tk, HEAD_W, tk), lambda h, i: (0, h, 0)),
            vec, vec, vec, vec,
            pl.BlockSpec((1, HEAD_W), lambda h, i: (0, 0)),
        ],
        out_specs=pl.BlockSpec((tq, HEAD_W), lambda h, i: (i, h)),
        out_shape=jax.ShapeDtypeStruct((rows, ATT_W), BF16),
        scratch_shapes=[pltpu.VMEM((2, tk, tq), F32), pltpu.VMEM((2, 1, tq), F32),
                        pltpu.VMEM((2, tk, tq), BF16), pltpu.VMEM((2, 1, tq), F32),
                        pltpu.VMEM((2, 1, tq), F32), pltpu.VMEM((2, 1, tq), F32),
                        pltpu.VMEM((2, HEAD_W, tq), F32)],
        compiler_params=pltpu.CompilerParams(
            dimension_semantics=("arbitrary", "arbitrary"), vmem_limit_bytes=_vmem_limit(est)),
        name="prompt_attn",
    )(qt, kbf, vt, *lam_vecs, subln_g)


def _sample_attn_kernel(q_ref, kc_ref, vc_ref, kn_ref, vn_ref, lq1_ref, lk1_ref, lq2_ref, lk2_ref,
                        g_ref, out_ref, *, past_len, lam_init):
    t = q_ref.shape[0]
    comp = lax.broadcasted_iota(jnp.int32, (t, HEAD_W), 1) < HEAD_DIM
    q_chunk = (past_len + lax.broadcasted_iota(jnp.int32, (t, 1), 0)) // CHUNK
    ok_c = lax.broadcasted_iota(jnp.int32, (t, past_len), 1) // CHUNK <= q_chunk
    ok_n = (past_len + lax.broadcasted_iota(jnp.int32, (t, t), 1)) // CHUNK <= q_chunk
    nt = (((1,), (1,)), ((), ()))
    lam = _lambda(lq1_ref, lk1_ref, lq2_ref, lk2_ref, lam_init)
    for h in range(N_HEADS):
        cols = slice(h * HEAD_W, (h + 1) * HEAD_W)
        q = q_ref[:, cols]
        zero = jnp.zeros_like(q)
        qs = (jnp.where(comp, q, zero), jnp.where(comp, zero, q))
        kc = kc_ref[0, :, cols].astype(BF16)
        vc = vc_ref[0, :, h, :].astype(BF16)
        kn = kn_ref[:, cols]
        vn = vn_ref[:, cols]
        outs = []
        for c in range(2):
            sc = lax.dot_general(qs[c], kc, nt, preferred_element_type=F32)
            sn = lax.dot_general(qs[c], kn, nt, preferred_element_type=F32)
            sc = jnp.where(ok_c, sc, -jnp.inf)
            sn = jnp.where(ok_n, sn, -jnp.inf)
            m = jnp.maximum(jnp.max(sc, axis=-1, keepdims=True), jnp.max(sn, axis=-1, keepdims=True))
            pc = jnp.exp2(sc - m)
            pn = jnp.exp2(sn - m)
            den = jnp.sum(pc, axis=-1, keepdims=True) + jnp.sum(pn, axis=-1, keepdims=True)
            num = (jnp.dot(pc.astype(BF16), vc, preferred_element_type=F32)
                   + jnp.dot(pn.astype(BF16), vn, preferred_element_type=F32))
            outs.append(num / den)
        o = outs[0] - lam * outs[1]
        out_ref[:, cols] = (_rms(o, g_ref[...]) * (1.0 - lam_init)).astype(BF16)


def _sample_attention(q, kbf, vbf, cache_k, cache_v, lam_vecs, subln_g, *, lam_init, t_new):
    n_streams, past_len = cache_k.shape[:2]
    vec = pl.BlockSpec((1, HEAD_DIM), lambda b: (0, 0))
    new_blk = pl.BlockSpec((t_new, ATT_W), lambda b: (b, 0))
    est = 2 * 2 * past_len * ATT_W * 4 + 16 * t_new * past_len * 4
    return pl.pallas_call(
        functools.partial(_sample_attn_kernel, past_len=past_len, lam_init=lam_init),
        grid=(n_streams,),
        in_specs=[new_blk, pl.BlockSpec((1, past_len, ATT_W), lambda b: (b, 0, 0)),
                  pl.BlockSpec((1, past_len, N_HEADS, HEAD_W), lambda b: (b, 0, 0, 0)),
                  new_blk, new_blk, vec, vec, vec, vec,
                  pl.BlockSpec((1, HEAD_W), lambda b: (0, 0))],
        out_specs=new_blk,
        out_shape=jax.ShapeDtypeStruct((n_streams * t_new, ATT_W), BF16),
        compiler_params=pltpu.CompilerParams(
            dimension_semantics=("arbitrary",), vmem_limit_bytes=_vmem_limit(est)),
        name="sample_attn",
    )(q, cache_k, cache_v, kbf, vbf, *lam_vecs, subln_g)


def _outproj_kernel(x_ref, a_ref, b_ref, w_ref, gpost_ref, gffn_ref, x1_ref, hn_ref):
    mix = (jnp.dot(a_ref[...], w_ref[:GMLP_W, :], preferred_element_type=F32)
           + jnp.dot(b_ref[...], w_ref[GMLP_W:, :], preferred_element_type=F32))
    x1 = x_ref[...] + _rms(mix, gpost_ref[...])
    x1_ref[...] = x1
    hn_ref[...] = _rms(x1, gffn_ref[...]).astype(BF16)


def _outproj(x, out_a, out_b, w_out, g_post, g_ffn, *, tm):
    rows, d = x.shape
    row_blk = lambda w: pl.BlockSpec((tm, w), lambda i: (i, 0))
    const = lambda shape: pl.BlockSpec(shape, lambda i: (0,) * len(shape))
    est = 2 * tm * d * (4 + 4 + 2) + 2 * 2 * tm * GMLP_W * 2 + 2 * w_out.size * 2 + 4 * tm * d * 4
    return pl.pallas_call(
        _outproj_kernel,
        grid=(rows // tm,),
        in_specs=[row_blk(d), row_blk(GMLP_W), row_blk(ATT_W), const(w_out.shape),
                  const((1, d)), const((1, d))],
        out_specs=[row_blk(d), row_blk(d)],
        out_shape=[jax.ShapeDtypeStruct((rows, d), F32), jax.ShapeDtypeStruct((rows, d), BF16)],
        compiler_params=pltpu.CompilerParams(
            dimension_semantics=("arbitrary",), vmem_limit_bytes=_vmem_limit(est)),
        name="outproj",
    )(x, out_a, out_b, w_out, g_post, g_ffn)


def _ffn_kernel(hn_ref, x1_ref, wg_ref, wv_ref, cwg_ref, cwv_ref, cbg_ref, cbv_ref, hg_ref, hv_ref,
                wd_ref, gpost_ref, y_ref, cg_ref, cv_ref, acc_s, carg_s, carv_s, stage_s,
                *, n_streams, stream_len):
    i = pl.program_id(0)
    j = pl.program_id(1)
    tm = hn_ref.shape[0]
    ft = wg_ref.shape[1]
    hn = hn_ref[...]
    sub = min(FF_SUBTILE, ft)

    @pl.when(j == 0)
    def _():
        acc_s[...] = jnp.zeros(acc_s.shape, F32)

    if n_streams == 1:
        @pl.when(i == 0)
        def _():
            pad = jnp.zeros((SUBLANES - (CONV_W - 1), ft), F32)
            carg_s[j] = jnp.concatenate([pad, hg_ref[0]], axis=0)
            carv_s[j] = jnp.concatenate([pad, hv_ref[0]], axis=0)

    def conv_branch(buf, cs, w_ref, cw_ref, cb_ref, h_ref, car_s, cout_ref):
        up = jnp.dot(hn, w_ref[:, cs], preferred_element_type=F32)
        taps = [[], [], []]
        for s in range(n_streams):
            lo = s * stream_len
            if n_streams == 1:
                stage_s[buf, s, 0:SUBLANES] = car_s[j][:, cs]
            else:
                stage_s[buf, s, SUBLANES - (CONV_W - 1):SUBLANES] = h_ref[s, :, cs]
            stage_s[buf, s, SUBLANES:SUBLANES + stream_len] = up[lo:lo + stream_len]
            for k in range(CONV_W):
                first = SUBLANES - (CONV_W - 1) + k
                taps[k].append(stage_s[buf, s, first:first + stream_len])
            cout_ref[0, s, :, cs] = stage_s[buf, s, SUBLANES + stream_len - (CONV_W - 1):
                                            SUBLANES + stream_len]
        if n_streams == 1:
            car_s[j, :, cs] = stage_s[buf, 0, stream_len:stream_len + SUBLANES]
        taps = [t[0] if n_streams == 1 else jnp.concatenate(t, axis=0) for t in taps]
        cw = cw_ref[:, cs]
        return taps[0] * cw[0:1] + taps[1] * cw[1:2] + taps[2] * cw[2:3] + cb_ref[:, cs]

    part = None
    for c0 in range(0, ft, sub):
        cs = slice(c0, c0 + sub)
        buf = 2 * (c0 // sub)
        gate = conv_branch(buf, cs, wg_ref, cwg_ref, cbg_ref, hg_ref, carg_s, cg_ref)
        val = conv_branch(buf + 1, cs, wv_ref, cwv_ref, cbv_ref, hv_ref, carv_s, cv_ref)
        act = (_gelu(gate) * val).astype(BF16)
        d_part = jnp.dot(act, wd_ref[cs, :], preferred_element_type=F32)
        part = d_part if part is None else part + d_part
    acc_s[...] += part

    @pl.when(j == pl.num_programs(1) - 1)
    def _():
        y_ref[...] = x1_ref[...] + _rms(acc_s[...], gpost_ref[...])


def _ffn(hn, x1, w_up, conv_w, conv_b, hist, w_down, g_post, *, tm, ft, n_streams):
    rows, d = x1.shape
    d_ff = w_down.shape[0]
    nj = d_ff // ft
    n_tiles = rows // tm
    stream_len = tm // n_streams
    sub = min(FF_SUBTILE, ft)
    assert rows % tm == 0 and d_ff % ft == 0 and (n_streams == 1 or n_tiles == 1)
    assert hist.shape == (n_streams, CONV_W - 1, 2 * d_ff) and stream_len % SUBLANES == 0
    gate_col = lambda i, j: (0, j)
    val_col = lambda i, j: (0, nj + j)
    row_blk = lambda w: pl.BlockSpec((tm, w), lambda i, j: (i, 0))
    once_per_tile = lambda w: pl.BlockSpec((tm, w), lambda i, j: (i, 0), pipeline_mode=pl.Buffered(1))
    hist_blk = lambda off: pl.BlockSpec((n_streams, CONV_W - 1, ft), lambda i, j: (0, 0, off + j))
    tail_blk = pl.BlockSpec((1, n_streams, CONV_W - 1, ft), lambda i, j: (i, 0, 0, j))
    tail_shape = jax.ShapeDtypeStruct((n_tiles, n_streams, CONV_W - 1, d_ff), F32)
    est = (tm * d * (2 + 4 + 2 * 4) + tm * d * 4 + 2 * 3 * d * ft * 2 + 16 * tm * FF_SUBTILE * 4
           + 2 * tm * d * 4)
    y, cg, cv = pl.pallas_call(
        functools.partial(_ffn_kernel, n_streams=n_streams, stream_len=stream_len),
        grid=(n_tiles, nj),
        in_specs=[
            once_per_tile(d), once_per_tile(d),
            pl.BlockSpec((d, ft), gate_col), pl.BlockSpec((d, ft), val_col),
            pl.BlockSpec((CONV_W, ft), gate_col), pl.BlockSpec((CONV_W, ft), val_col),
            pl.BlockSpec((1, ft), gate_col), pl.BlockSpec((1, ft), val_col),
            hist_blk(0), hist_blk(nj),
            pl.BlockSpec((ft, d), lambda i, j: (j, 0)),
            pl.BlockSpec((1, d), lambda i, j: (0, 0)),
        ],
        out_specs=[row_blk(d), tail_blk, tail_blk],
        out_shape=[jax.ShapeDtypeStruct((rows, d), F32), tail_shape, tail_shape],
        scratch_shapes=[pltpu.VMEM((tm, d), F32),
                        pltpu.VMEM((nj, SUBLANES, ft), F32), pltpu.VMEM((nj, SUBLANES, ft), F32),
                        pltpu.VMEM((2 * (ft // sub), n_streams, SUBLANES + stream_len, sub), F32)],
        compiler_params=pltpu.CompilerParams(
            dimension_semantics=("arbitrary", "arbitrary"), vmem_limit_bytes=_vmem_limit(est)),
        name="convffn",
    )(hn, x1, w_up, w_up, conv_w, conv_w, conv_b, conv_b, hist, hist, w_down, g_post)
    return y, jnp.concatenate([cg[-1], cv[-1]], axis=-1)


def _layer(x, pos, hist, attend, lw, *, gmlp_len, transposed, emit_vn, n_streams, tm):
    (g_mix_pre, w_in, gv, w_s, b_s, subln_g, w_out, g_mix_post, g_ffn_pre, w_up, conv_w, conv_b,
     w_down, g_ffn_post) = lw
    res = _inproj(x, pos, g_mix_pre, w_in, gv, w_s, b_s, gmlp_len=gmlp_len, transposed=transposed,
                  emit_vn=emit_vn, tm=min(PROJ_ROW_TILE, tm))
    out_a, q, k32, kbf, v32, vbf = res[:6]
    vn = res[6] if emit_vn else None
    out_b = attend(q, kbf, vbf)
    x1, hn2 = _outproj(x, out_a, out_b, w_out, g_mix_post, g_ffn_pre, tm=tm)
    y, new_hist = _ffn(hn2, x1, w_up, conv_w, conv_b, hist, w_down, g_ffn_post,
                       tm=tm, ft=FF_TILE, n_streams=n_streams)
    return y, k32, v32, vn, new_hist


def kernel(x_prompt, x_sample, cache_k, cache_v, state_conv, g_mix_pre, w_in, gmlp_v_gain, gmlp_w_s,
           gmlp_b_s, lambda_q1, lambda_k1, lambda_q2, lambda_k2, subln_g, w_out, g_mix_post,
           g_ffn_pre, w_up, conv_w, conv_b, w_down, g_ffn_post):
    b_p, s_p, d = x_prompt.shape
    b_s, t_new, _ = x_sample.shape
    depth = w_in.shape[0]
    past_len = cache_k.shape[2]
    d_ff = w_down.shape[1]
    assert b_p == 1
    pos_p = (0, s_p, 1)
    pos_s = (past_len, t_new, b_s)
    zero_hist = jnp.zeros((b_p, CONV_W - 1, 2 * d_ff), F32)
    row = lambda a: a.reshape(1, -1)

    yp = x_prompt.reshape(b_p * s_p, d)
    ys = x_sample.reshape(b_s * t_new, d)
    outs = [[] for _ in range(7)]
    for l in range(depth):
        lam_init = 0.8 - 0.6 * math.exp(-0.3 * l)
        lam_vecs = tuple(row(a[l]) for a in (lambda_q1, lambda_k1, lambda_q2, lambda_k2))
        sub_g = row(subln_g[l])
        lw = (row(g_mix_pre[l]), w_in[l].astype(BF16), row(gmlp_v_gain[l]), gmlp_w_s[l], gmlp_b_s[l],
              sub_g, w_out[l].astype(BF16), row(g_mix_post[l]), row(g_ffn_pre[l]),
              w_up[l].astype(BF16), conv_w[l], row(conv_b[l]), w_down[l].astype(BF16),
              row(g_ffn_post[l]))

        tq = min(ATT_Q_TILE, s_p)
        attend_p = lambda q, k, v: _prompt_attention(
            q, k, v, lam_vecs, sub_g, lam_init=lam_init, tq=tq, tk=min(ATT_K_TILE, tq))
        yp, kp, vp, _, cp = _layer(yp, pos_p, zero_hist, attend_p, lw, gmlp_len=GROUP_DIM,
                                   transposed=True, emit_vn=False, n_streams=1,
                                   tm=min(ROW_TILE, s_p))

        ck = cache_k[l].reshape(b_s, past_len, ATT_W)
        cv = cache_v[l]
        attend_s = lambda q, k, v: _sample_attention(
            q, k, v, ck, cv, lam_vecs, sub_g, lam_init=lam_init, t_new=t_new)
        ys, ks, vs, gs, cs = _layer(ys, pos_s, state_conv[l], attend_s, lw, gmlp_len=t_new,
                                    transposed=False, emit_vn=True, n_streams=b_s,
                                    tm=b_s * t_new)

        for acc, val in zip(outs, (
                kp.reshape(b_p, s_p, N_HEADS, 2, HEAD_DIM), vp.reshape(b_p, s_p, N_HEADS, HEAD_W), cp,
                ks.reshape(b_s, t_new, N_HEADS, 2, HEAD_DIM), vs.reshape(b_s, t_new, N_HEADS, HEAD_W),
                gs.reshape(b_s, t_new, N_GROUPS, GROUP_DIM), cs)):
            acc.append(val)

    return (yp.reshape(b_p, s_p, d), ys.reshape(b_s, t_new, d)) + tuple(jnp.stack(o) for o in outs)
```

```python
import functools
import math

import jax
import jax.numpy as jnp
from jax import lax
from jax.experimental import pallas as pl
from jax.experimental.pallas import tpu as pltpu

F32 = jnp.float32
BF16 = jnp.bfloat16

N_HEADS = 8
HEAD_DIM = 64
HEAD_W = 2 * HEAD_DIM
CHUNK = 64
ROT_DIM = HEAD_DIM // 4
ROT_HALF = ROT_DIM // 2
ROPE_THETA = 500000.0
ATT_SCALE = HEAD_DIM ** -0.5
N_GROUPS = 8
GROUP_DIM = 128
GMLP_W = N_GROUPS * GROUP_DIM
ATT_W = N_HEADS * HEAD_W
CONV_W = 3
EPS = 1e-6
LOG2E = 1.4426950408889634

LANES = 128
SUBLANES = 8
VMEM_BYTES_V7X = 64 * 1024 * 1024

ROW_TILE = 512
PROJ_ROW_TILE = 256
FF_TILE = 512
FF_SUBTILE = 256
CONV_PHASES = 4
ATT_Q_TILE = 512
ATT_K_TILE = 512


def _vmem_limit(estimate_bytes):
    return int(min(VMEM_BYTES_V7X - (4 << 20), max(32 << 20, estimate_bytes)))


def _gelu(x):
    a = -2.0 * math.sqrt(2.0 / math.pi) * LOG2E
    return x / (1.0 + jnp.exp2(x * (a + (a * 0.044715) * (x * x))))


def _rms(x, g):
    return x * lax.rsqrt(jnp.mean(x * x, axis=-1, keepdims=True) + EPS) * g


def _rope(z, cos, sin_lo, sin_hi):
    outs = []
    for s in range(z.shape[1] // LANES):
        xs = z[:, s * LANES:(s + 1) * LANES]
        nxt = pltpu.roll(xs, LANES - ROT_HALF, axis=1)
        prv = pltpu.roll(xs, ROT_HALF, axis=1)
        outs.append(xs * cos + nxt * sin_lo + prv * sin_hi)
    return jnp.concatenate(outs, axis=1)


def _inproj_kernel(x_ref, gpre_ref, w_ref, gv_ref, ws_ref, bs_ref, cos_ref, slo_ref, shi_ref,
                   *refs, gmlp_len, transposed, emit_vn, k_tile):
    if emit_vn:
        outa_ref, q_ref, k32_ref, kbf_ref, v32_ref, vbf_ref, vn_ref = refs
    else:
        outa_ref, q_ref, k32_ref, kbf_ref, v32_ref, vbf_ref = refs
        vn_ref = None
    tm = x_ref.shape[0]
    hn = _rms(x_ref[...], gpre_ref[...]).astype(BF16)
    block = lambda n: jnp.dot(hn, w_ref[:, n * GMLP_W:(n + 1) * GMLP_W],
                              preferred_element_type=F32)

    u = _gelu(block(0))
    vn = _rms(_gelu(block(1)), gv_ref[...])
    if emit_vn:
        vn_ref[...] = vn
    vnb = vn.astype(BF16)
    ln = gmlp_len
    causal = (lax.broadcasted_iota(jnp.int32, (ln, ln), 1)
              <= lax.broadcasted_iota(jnp.int32, (ln, ln), 0))
    for g in range(N_GROUPS):
        wg = jnp.where(causal, ws_ref[g], 0.0).astype(BF16)
        cols = slice(g * GROUP_DIM, (g + 1) * GROUP_DIM)
        bias = bs_ref[:, cols]
        for ch in range(tm // ln):
            rows = slice(ch * ln, (ch + 1) * ln)
            sg = jnp.dot(wg, vnb[rows, cols], preferred_element_type=F32) + bias
            outa_ref[rows, cols] = (u[rows, cols] * sg).astype(BF16)

    cos, slo, shi = cos_ref[...], slo_ref[...], shi_ref[...]
    q = _rope(block(2), cos, slo, shi) * (ATT_SCALE * LOG2E)
    q_ref[...] = q.T.astype(BF16) if transposed else q.astype(BF16)

    k = _rope(block(3), cos, slo, shi)
    k32_ref[...] = k
    kbf_ref[...] = k.astype(BF16)

    v = block(4)
    v32_ref[...] = v
    if not transposed:
        vbf_ref[...] = v.astype(BF16)
    else:
        vt = v.T.astype(BF16)
        if k_tile >= tm:
            vbf_ref[0] = vt
        else:
            for cb in range(tm // k_tile):
                vbf_ref[cb] = vt[:, cb * k_tile:(cb + 1) * k_tile]


def _rope_tables(start, count):
    lane = jnp.arange(LANES)
    in_block = lane % HEAD_DIM
    inv_freq = ROPE_THETA ** (-jnp.arange(ROT_HALF, dtype=F32) / ROT_HALF)
    freq = inv_freq[lane % ROT_HALF][None, :]
    a0 = start // LANES
    n_hi = -(-(start + count) // LANES) - a0
    hi = ((a0 + jnp.arange(n_hi)) * LANES).astype(F32)[:, None] * freq
    lo = jnp.arange(LANES).astype(F32)[:, None] * freq
    cos_hi, sin_hi = jnp.cos(hi)[:, None, :], jnp.sin(hi)[:, None, :]
    cos_lo, sin_lo = jnp.cos(lo)[None], jnp.sin(lo)[None]
    first = start - a0 * LANES
    rows = lambda t: t.reshape(n_hi * LANES, LANES)[first:first + count]
    cos = rows(cos_hi * cos_lo - sin_hi * sin_lo)
    sin = rows(sin_hi * cos_lo + cos_hi * sin_lo)
    rot_lo = in_block < ROT_HALF
    rot_hi = (in_block >= ROT_HALF) & (in_block < ROT_DIM)
    return (jnp.where(rot_lo | rot_hi, cos, 1.0), jnp.where(rot_lo, -sin, 0.0),
            jnp.where(rot_hi, sin, 0.0))


def _inproj(x, pos, g_pre, w_in, gv, w_s, b_s, *, gmlp_len, transposed, emit_vn, tm):
    rows, d = x.shape
    assert rows % tm == 0 and tm % gmlp_len == 0
    k_tile = min(ATT_K_TILE, rows)
    assert k_tile % tm == 0 or tm % k_tile == 0
    start, count, repeats = pos
    assert count * repeats == rows
    cos, slo, shi = (jnp.tile(t, (repeats, 1)) for t in _rope_tables(start, count))
    ws = w_s[:, :gmlp_len, :gmlp_len]
    bs = jnp.repeat(b_s[:, :gmlp_len].T, GROUP_DIM, axis=1)

    row_blk = lambda w: pl.BlockSpec((tm, w), lambda i: (i, 0))
    const = lambda shape: pl.BlockSpec(shape, lambda i: (0,) * len(shape), pipeline_mode=pl.Buffered(1))
    in_specs = [
        row_blk(d), const((1, d)), const(w_in.shape),
        const((1, GMLP_W)), const(ws.shape), const(bs.shape),
        row_blk(LANES), row_blk(LANES), row_blk(LANES),
    ]
    if transposed:
        q_shape, q_spec = (ATT_W, rows), pl.BlockSpec((ATT_W, tm), lambda i: (0, i))
        v_shape = (rows // k_tile, ATT_W, k_tile)
        if k_tile >= tm:
            per = k_tile // tm
            v_spec = pl.BlockSpec((1, ATT_W, tm), lambda i: (i // per, 0, i % per))
        else:
            v_spec = pl.BlockSpec((tm // k_tile, ATT_W, k_tile), lambda i: (i, 0, 0))
    else:
        q_shape, q_spec = (rows, ATT_W), row_blk(ATT_W)
        v_shape, v_spec = (rows, ATT_W), row_blk(ATT_W)
    out_shape = [
        jax.ShapeDtypeStruct((rows, GMLP_W), BF16), jax.ShapeDtypeStruct(q_shape, BF16),
        jax.ShapeDtypeStruct((rows, ATT_W), F32), jax.ShapeDtypeStruct((rows, ATT_W), BF16),
        jax.ShapeDtypeStruct((rows, ATT_W), F32), jax.ShapeDtypeStruct(v_shape, BF16),
    ]
    out_specs = [row_blk(GMLP_W), q_spec, row_blk(ATT_W), row_blk(ATT_W), row_blk(ATT_W), v_spec]
    if emit_vn:
        out_shape.append(jax.ShapeDtypeStruct((rows, GMLP_W), F32))
        out_specs.append(row_blk(GMLP_W))
    est = (2 * tm * d * 4 + w_in.size * 2 + tm * d * 2
           + 2 * tm * GMLP_W * (2 + 2 + 4 + 2 + 4 + 2 + 4) + 12 * tm * GMLP_W * 4)
    return pl.pallas_call(
        functools.partial(_inproj_kernel, gmlp_len=gmlp_len, transposed=transposed,
                          emit_vn=emit_vn, k_tile=k_tile),
        grid=(rows // tm,),
        in_specs=in_specs, out_specs=out_specs, out_shape=out_shape,
        compiler_params=pltpu.CompilerParams(
            dimension_semantics=("arbitrary",), vmem_limit_bytes=_vmem_limit(est)),
        name="inproj",
    )(x, g_pre, w_in, gv, ws, bs, cos, slo, shi)


def _lambda(lq1_ref, lk1_ref, lq2_ref, lk2_ref, lam_init):
    d1 = jnp.sum(lq1_ref[...] * lk1_ref[...], axis=-1, keepdims=True)
    d2 = jnp.sum(lq2_ref[...] * lk2_ref[...], axis=-1, keepdims=True)
    return jnp.exp(d1) - jnp.exp(d2) + lam_init


def _prompt_attn_kernel(qt_ref, k_ref, vt_ref, lq1_ref, lk1_ref, lq2_ref, lk2_ref, g_ref, out_ref,
                        s_buf, smax_buf, p_buf, alpha_buf, m_s, l_s, acc_s, *, tq, tk, lam_init):
    qi = pl.program_id(1)
    qt = qt_ref[...]
    comp = lax.broadcasted_iota(jnp.int32, qt.shape, 0) < HEAD_DIM
    zero = jnp.zeros_like(qt)
    qs = (jnp.where(comp, qt, zero), jnp.where(comp, zero, qt))

    m_s[...] = jnp.full(m_s.shape, -jnp.inf, F32)
    l_s[...] = jnp.zeros(l_s.shape, F32)
    acc_s[...] = jnp.zeros(acc_s.shape, F32)

    steps_per_tile = tq // tk
    n_full = qi * steps_per_tile
    n_total = n_full + steps_per_tile

    def block_of(j):
        if isinstance(j, int):
            return n_full + j if j < steps_per_tile else j - steps_per_tile
        return jnp.where(j < steps_per_tile, n_full + j, j - steps_per_tile)

    def scores(j):
        kb = block_of(j)
        kblk = k_ref[pl.ds(pl.multiple_of(kb * tk, tk), tk), :]
        masked = isinstance(j, int) and j < steps_per_tile
        if masked:
            key_chunk = (lax.broadcasted_iota(jnp.int32, (tk, tq), 0) + j * tk) // CHUNK
            allowed = key_chunk <= lax.broadcasted_iota(jnp.int32, (tk, tq), 1) // CHUNK
        for c in range(2):
            s = jnp.dot(kblk, qs[c], preferred_element_type=F32)
            if masked:
                s = jnp.where(allowed, s, -jnp.inf)
            s_buf[c] = s
            smax_buf[c] = jnp.max(s, axis=0, keepdims=True)

    def softmax():
        for c in range(2):
            m_old = m_s[c]
            m_new = jnp.maximum(m_old, smax_buf[c])
            alpha = jnp.exp2(m_old - m_new)
            alpha_buf[c] = alpha
            p = jnp.exp2(s_buf[c] - m_new)
            l_s[c] = alpha * l_s[c] + jnp.sum(p, axis=0, keepdims=True)
            p_buf[c] = p.astype(BF16)
            m_s[c] = m_new

    def values(j):
        vtb = vt_ref[block_of(j)]
        for c in range(2):
            acc_s[c] = alpha_buf[c] * acc_s[c] + jnp.dot(vtb, p_buf[c], preferred_element_type=F32)

    def step(u, with_values=True, with_softmax=True):
        if with_values:
            values(u - 2)
        if with_softmax:
            softmax()
        scores(u)

    def run_steps(first, stop):
        count = stop - first

        def pair(v, carry):
            step(first + 2 * v)
            step(first + 2 * v + 1)
            return carry

        lax.fori_loop(0, count // 2, pair, 0)
        pl.when(count % 2 == 1)(lambda: step(stop - 1))

    def drain():
        values(n_total - 2)
        softmax()

    if steps_per_tile >= 2:
        for u in range(steps_per_tile):
            step(u, with_values=u >= 2, with_softmax=u >= 1)
        run_steps(steps_per_tile, n_total)
        drain()
        values(n_total - 1)
    else:
        @pl.when(qi > 0)
        def _():
            step(0, with_values=False, with_softmax=False)
            step(1, with_values=False)
            run_steps(2, n_total)
            drain()
            values(n_total - 1)

        @pl.when(qi == 0)
        def _():
            step(0, with_values=False, with_softmax=False)
            softmax()
            values(0)

    lam = _lambda(lq1_ref, lk1_ref, lq2_ref, lk2_ref, lam_init)
    o = acc_s[0] / l_s[0] - lam * (acc_s[1] / l_s[1])
    on = o * lax.rsqrt(jnp.mean(o * o, axis=0, keepdims=True) + EPS)
    out_ref[...] = (on.T * (g_ref[...] * (1.0 - lam_init))).astype(BF16)


def _prompt_attention(qt, kbf, vt, lam_vecs, subln_g, *, lam_init, tq, tk):
    rows = kbf.shape[0]
    assert rows % tq == 0 and tq % tk == 0 and tk % CHUNK == 0 and vt.shape[2] == tk
    vec = pl.BlockSpec((1, HEAD_DIM), lambda h, i: (0, 0))
    est = 2 * 2 * rows * HEAD_W * 2 + 2 * HEAD_W * tq * 4 + 12 * tk * tq * 4
    return pl.pallas_call(
        functools.partial(_prompt_attn_kernel, tq=tq, tk=tk, lam_init=lam_init),
        grid=(N_HEADS, rows // tq),
        in_specs=[
            pl.BlockSpec((HEAD_W, tq), lambda h, i: (h, i)),
            pl.BlockSpec((rows, HEAD_W), lambda h, i: (0, h)),
            pl.BlockSpec((rows // tk, HEAD_W, tk), lambda h, i: (0, h, 0)),
            vec, vec, vec, vec,
            pl.BlockSpec((1, HEAD_W), lambda h, i: (0, 0)),
        ],
        out_specs=pl.BlockSpec((tq, HEAD_W), lambda h, i: (i, h)),
        out_shape=jax.ShapeDtypeStruct((rows, ATT_W), BF16),
        scratch_shapes=[pltpu.VMEM((2, tk, tq), F32), pltpu.VMEM((2, 1, tq), F32),
                        pltpu.VMEM((2, tk, tq), BF16), pltpu.VMEM((2, 1, tq), F32),
                        pltpu.VMEM((2, 1, tq), F32), pltpu.VMEM((2, 1, tq), F32),
                        pltpu.VMEM((2, HEAD_W, tq), F32)],
        compiler_params=pltpu.CompilerParams(
            dimension_semantics=("arbitrary", "arbitrary"), vmem_limit_bytes=_vmem_limit(est)),
        name="prompt_attn",
    )(qt, kbf, vt, *lam_vecs, subln_g)


def _sample_attn_kernel(q_ref, kc_ref, vc_ref, kn_ref, vn_ref, lq1_ref, lk1_ref, lq2_ref, lk2_ref,
                        g_ref, out_ref, *, past_len, lam_init):
    t = q_ref.shape[0]
    comp = lax.broadcasted_iota(jnp.int32, (t, HEAD_W), 1) < HEAD_DIM
    q_chunk = (past_len + lax.broadcasted_iota(jnp.int32, (t, 1), 0)) // CHUNK
    ok_c = lax.broadcasted_iota(jnp.int32, (t, past_len), 1) // CHUNK <= q_chunk
    ok_n = (past_len + lax.broadcasted_iota(jnp.int32, (t, t), 1)) // CHUNK <= q_chunk
    nt = (((1,), (1,)), ((), ()))
    lam = _lambda(lq1_ref, lk1_ref, lq2_ref, lk2_ref, lam_init)
    for h in range(N_HEADS):
        cols = slice(h * HEAD_W, (h + 1) * HEAD_W)
        q = q_ref[:, cols]
        zero = jnp.zeros_like(q)
        qs = (jnp.where(comp, q, zero), jnp.where(comp, zero, q))
        kc = kc_ref[0, :, cols].astype(BF16)
        vc = vc_ref[0, :, h, :].astype(BF16)
        kn = kn_ref[:, cols]
        vn = vn_ref[:, cols]
        outs = []
        for c in range(2):
            sc = lax.dot_general(qs[c], kc, nt, preferred_element_type=F32)
            sn = lax.dot_general(qs[c], kn, nt, preferred_element_type=F32)
            sc = jnp.where(ok_c, sc, -jnp.inf)
            sn = jnp.where(ok_n, sn, -jnp.inf)
            m = jnp.maximum(jnp.max(sc, axis=-1, keepdims=True), jnp.max(sn, axis=-1, keepdims=True))
            pc = jnp.exp2(sc - m)
            pn = jnp.exp2(sn - m)
            den = jnp.sum(pc, axis=-1, keepdims=True) + jnp.sum(pn, axis=-1, keepdims=True)
            num = (jnp.dot(pc.astype(BF16), vc, preferred_element_type=F32)
                   + jnp.dot(pn.astype(BF16), vn, preferred_element_type=F32))
            outs.append(num / den)
        o = outs[0] - lam * outs[1]
        out_ref[:, cols] = (_rms(o, g_ref[...]) * (1.0 - lam_init)).astype(BF16)


def _sample_attention(q, kbf, vbf, cache_k, cache_v, lam_vecs, subln_g, *, lam_init, t_new):
    n_streams, past_len = cache_k.shape[:2]
    vec = pl.BlockSpec((1, HEAD_DIM), lambda b: (0, 0))
    new_blk = pl.BlockSpec((t_new, ATT_W), lambda b: (b, 0))
    est = 2 * 2 * past_len * ATT_W * 4 + 16 * t_new * past_len * 4
    return pl.pallas_call(
        functools.partial(_sample_attn_kernel, past_len=past_len, lam_init=lam_init),
        grid=(n_streams,),
        in_specs=[new_blk, pl.BlockSpec((1, past_len, ATT_W), lambda b: (b, 0, 0)),
                  pl.BlockSpec((1, past_len, N_HEADS, HEAD_W), lambda b: (b, 0, 0, 0)),
                  new_blk, new_blk, vec, vec, vec, vec,
                  pl.BlockSpec((1, HEAD_W), lambda b: (0, 0))],
        out_specs=new_blk,
        out_shape=jax.ShapeDtypeStruct((n_streams * t_new, ATT_W), BF16),
        compiler_params=pltpu.CompilerParams(
            dimension_semantics=("arbitrary",), vmem_limit_bytes=_vmem_limit(est)),
        name="sample_attn",
    )(q, cache_k, cache_v, kbf, vbf, *lam_vecs, subln_g)


def _outproj_kernel(x_ref, a_ref, b_ref, w_ref, gpost_ref, gffn_ref, x1_ref, hn_ref):
    mix = (jnp.dot(a_ref[...], w_ref[:GMLP_W, :], preferred_element_type=F32)
           + jnp.dot(b_ref[...], w_ref[GMLP_W:, :], preferred_element_type=F32))
    x1 = x_ref[...] + _rms(mix, gpost_ref[...])
    x1_ref[...] = x1
    hn_ref[...] = _rms(x1, gffn_ref[...]).astype(BF16)


def _outproj(x, out_a, out_b, w_out, g_post, g_ffn, *, tm):
    rows, d = x.shape
    row_blk = lambda w: pl.BlockSpec((tm, w), lambda i: (i, 0))
    const = lambda shape: pl.BlockSpec(shape, lambda i: (0,) * len(shape))
    est = 2 * tm * d * (4 + 4 + 2) + 2 * 2 * tm * GMLP_W * 2 + 2 * w_out.size * 2 + 4 * tm * d * 4
    return pl.pallas_call(
        _outproj_kernel,
        grid=(rows // tm,),
        in_specs=[row_blk(d), row_blk(GMLP_W), row_blk(ATT_W), const(w_out.shape),
                  const((1, d)), const((1, d))],
        out_specs=[row_blk(d), row_blk(d)],
        out_shape=[jax.ShapeDtypeStruct((rows, d), F32), jax.ShapeDtypeStruct((rows, d), BF16)],
        compiler_params=pltpu.CompilerParams(
            dimension_semantics=("arbitrary",), vmem_limit_bytes=_vmem_limit(est)),
        name="outproj",
    )(x, out_a, out_b, w_out, g_post, g_ffn)


def _ffn_kernel(hn_ref, x1_ref, wg_ref, wv_ref, cwg_ref, cwv_ref, cbg_ref, cbv_ref, hg_ref, hv_ref,
                wd_ref, gpost_ref, y_ref, cg_ref, cv_ref, acc_s, carg_s, carv_s, stage_s, order_s,
                *, n_streams, stream_len):
    i = pl.program_id(0)
    j = pl.program_id(1)
    tm = hn_ref.shape[0]
    ft = wg_ref.shape[1]
    hn = hn_ref[...]
    sub = min(FF_SUBTILE, ft)
    slabs = sub // LANES
    per_phase = stream_len // CONV_PHASES

    @pl.when(j == 0)
    def _():
        acc_s[...] = jnp.zeros(acc_s.shape, F32)

    if n_streams == 1:
        @pl.when(i == 0)
        def _():
            pad = jnp.zeros((SUBLANES - (CONV_W - 1), ft), F32)
            carg_s[j] = jnp.concatenate([pad, hg_ref[0]], axis=0)
            carv_s[j] = jnp.concatenate([pad, hv_ref[0]], axis=0)

    def conv_branch(buf, cs, w_ref, cw_ref, cb_ref, h_ref, car_s, cout_ref):
        up = jnp.dot(hn, w_ref[:, cs], preferred_element_type=F32)
        cw = cw_ref[:, cs]
        cb = cb_ref[:, cs]
        hist = car_s[j][:, cs] if n_streams == 1 else None
        halves = []
        for half in range(slabs):
            slab = buf * slabs + half
            lanes = slice(half * LANES, (half + 1) * LANES)
            out_lanes = slice(cs.start + half * LANES, cs.start + (half + 1) * LANES)
            for s in range(n_streams):
                if n_streams == 1:
                    stage_s[slab, s, 0:SUBLANES] = hist[:, lanes]
                else:
                    stage_s[slab, s, SUBLANES - (CONV_W - 1):SUBLANES] = h_ref[s, :, out_lanes]
                stage_s[slab, s, SUBLANES:SUBLANES + stream_len] = (
                    up[s * stream_len:(s + 1) * stream_len, lanes])
            phases = []
            for ph in range(CONV_PHASES):
                y = cb[:, lanes]
                for k in range(CONV_W):
                    first = SUBLANES - (CONV_W - 1) + k + ph
                    rows = [stage_s[slab, s, pl.ds(first, per_phase, stride=CONV_PHASES), :]
                            for s in range(n_streams)]
                    tap = rows[0] if n_streams == 1 else jnp.concatenate(rows, axis=0)
                    y = y + tap * cw[k:k + 1, lanes]
                phases.append(y)
            halves.append(jnp.concatenate(phases, axis=0))
            for s in range(n_streams):
                cout_ref[0, s, :, out_lanes] = stage_s[slab, s, SUBLANES + stream_len - (CONV_W - 1):
                                                       SUBLANES + stream_len]
            if n_streams == 1:
                car_s[j, :, out_lanes] = stage_s[slab, 0, stream_len:stream_len + SUBLANES]
        return jnp.concatenate(halves, axis=1)

    part = None
    for c0 in range(0, ft, sub):
        cs = slice(c0, c0 + sub)
        buf = 2 * (c0 // sub)
        gate = conv_branch(buf, cs, wg_ref, cwg_ref, cbg_ref, hg_ref, carg_s, cg_ref)
        val = conv_branch(buf + 1, cs, wv_ref, cwv_ref, cbv_ref, hv_ref, carv_s, cv_ref)
        act = (_gelu(gate) * val).astype(BF16)
        d_part = jnp.dot(act, wd_ref[cs, :], preferred_element_type=F32)
        part = d_part if part is None else part + d_part
    acc_s[...] += part

    @pl.when(j == pl.num_programs(1) - 1)
    def _():
        fn = _rms(acc_s[...], gpost_ref[...])
        rows_per_phase = tm // CONV_PHASES
        for c in range(fn.shape[1] // LANES):
            for ph in range(CONV_PHASES):
                order_s[c, pl.ds(ph, rows_per_phase, stride=CONV_PHASES), :] = (
                    fn[ph * rows_per_phase:(ph + 1) * rows_per_phase, c * LANES:(c + 1) * LANES])
        f_nat = jnp.concatenate([order_s[c] for c in range(fn.shape[1] // LANES)], axis=1)
        y_ref[...] = x1_ref[...] + f_nat


def _ffn(hn, x1, w_up, conv_w, conv_b, hist, w_down, g_post, *, tm, ft, n_streams):
    rows, d = x1.shape
    d_ff = w_down.shape[0]
    nj = d_ff // ft
    n_tiles = rows // tm
    stream_len = tm // n_streams
    sub = min(FF_SUBTILE, ft)
    assert rows % tm == 0 and d_ff % ft == 0 and (n_streams == 1 or n_tiles == 1)
    assert hist.shape == (n_streams, CONV_W - 1, 2 * d_ff)
    assert stream_len % (SUBLANES * CONV_PHASES) == 0
    gate_col = lambda i, j: (0, j)
    val_col = lambda i, j: (0, nj + j)
    row_blk = lambda w: pl.BlockSpec((tm, w), lambda i, j: (i, 0))
    once_per_tile = lambda w: pl.BlockSpec((tm, w), lambda i, j: (i, 0), pipeline_mode=pl.Buffered(1))
    hist_blk = lambda off: pl.BlockSpec((n_streams, CONV_W - 1, ft), lambda i, j: (0, 0, off + j))
    tail_blk = pl.BlockSpec((1, n_streams, CONV_W - 1, ft), lambda i, j: (i, 0, 0, j))
    tail_shape = jax.ShapeDtypeStruct((n_tiles, n_streams, CONV_W - 1, d_ff), F32)
    est = (tm * d * (2 + 4 + 2 * 4) + tm * d * 4 + 2 * 3 * d * ft * 2 + 16 * tm * FF_SUBTILE * 4
           + 2 * tm * d * 4)
    y, cg, cv = pl.pallas_call(
        functools.partial(_ffn_kernel, n_streams=n_streams, stream_len=stream_len),
        grid=(n_tiles, nj),
        in_specs=[
            once_per_tile(d), once_per_tile(d),
            pl.BlockSpec((d, ft), gate_col), pl.BlockSpec((d, ft), val_col),
            pl.BlockSpec((CONV_W, ft), gate_col), pl.BlockSpec((CONV_W, ft), val_col),
            pl.BlockSpec((1, ft), gate_col), pl.BlockSpec((1, ft), val_col),
            hist_blk(0), hist_blk(nj),
            pl.BlockSpec((ft, d), lambda i, j: (j, 0)),
            pl.BlockSpec((1, d), lambda i, j: (0, 0)),
        ],
        out_specs=[row_blk(d), tail_blk, tail_blk],
        out_shape=[jax.ShapeDtypeStruct((rows, d), F32), tail_shape, tail_shape],
        scratch_shapes=[pltpu.VMEM((tm, d), F32),
                        pltpu.VMEM((nj, SUBLANES, ft), F32), pltpu.VMEM((nj, SUBLANES, ft), F32),
                        pltpu.VMEM((2 * (ft // sub) * (sub // LANES), n_streams,
                                    SUBLANES + stream_len, LANES), F32),
                        pltpu.VMEM((d // LANES, tm, LANES), F32)],
        compiler_params=pltpu.CompilerParams(
            dimension_semantics=("arbitrary", "arbitrary"), vmem_limit_bytes=_vmem_limit(est)),
        name="convffn",
    )(hn, x1, w_up, w_up, conv_w, conv_w, conv_b, conv_b, hist, hist, w_down, g_post)
    return y, jnp.concatenate([cg[-1], cv[-1]], axis=-1)


def _layer(x, pos, hist, attend, lw, *, gmlp_len, transposed, emit_vn, n_streams, tm):
    (g_mix_pre, w_in, gv, w_s, b_s, subln_g, w_out, g_mix_post, g_ffn_pre, w_up, conv_w, conv_b,
     w_down, g_ffn_post) = lw
    res = _inproj(x, pos, g_mix_pre, w_in, gv, w_s, b_s, gmlp_len=gmlp_len, transposed=transposed,
                  emit_vn=emit_vn, tm=min(PROJ_ROW_TILE, tm))
    out_a, q, k32, kbf, v32, vbf = res[:6]
    vn = res[6] if emit_vn else None
    out_b = attend(q, kbf, vbf)
    x1, hn2 = _outproj(x, out_a, out_b, w_out, g_mix_post, g_ffn_pre, tm=tm)
    y, new_hist = _ffn(hn2, x1, w_up, conv_w, conv_b, hist, w_down, g_ffn_post,
                       tm=tm, ft=FF_TILE, n_streams=n_streams)
    return y, k32, v32, vn, new_hist


def kernel(x_prompt, x_sample, cache_k, cache_v, state_conv, g_mix_pre, w_in, gmlp_v_gain, gmlp_w_s,
           gmlp_b_s, lambda_q1, lambda_k1, lambda_q2, lambda_k2, subln_g, w_out, g_mix_post,
           g_ffn_pre, w_up, conv_w, conv_b, w_down, g_ffn_post):
    b_p, s_p, d = x_prompt.shape
    b_s, t_new, _ = x_sample.shape
    depth = w_in.shape[0]
    past_len = cache_k.shape[2]
    d_ff = w_down.shape[1]
    assert b_p == 1
    pos_p = (0, s_p, 1)
    pos_s = (past_len, t_new, b_s)
    zero_hist = jnp.zeros((b_p, CONV_W - 1, 2 * d_ff), F32)
    row = lambda a: a.reshape(1, -1)

    yp = x_prompt.reshape(b_p * s_p, d)
    ys = x_sample.reshape(b_s * t_new, d)
    outs = [[] for _ in range(7)]
    for l in range(depth):
        lam_init = 0.8 - 0.6 * math.exp(-0.3 * l)
        lam_vecs = tuple(row(a[l]) for a in (lambda_q1, lambda_k1, lambda_q2, lambda_k2))
        sub_g = row(subln_g[l])
        lw = (row(g_mix_pre[l]), w_in[l].astype(BF16), row(gmlp_v_gain[l]), gmlp_w_s[l], gmlp_b_s[l],
              sub_g, w_out[l].astype(BF16), row(g_mix_post[l]), row(g_ffn_pre[l]),
              w_up[l].astype(BF16), conv_w[l], row(conv_b[l]), w_down[l].astype(BF16),
              row(g_ffn_post[l]))

        tq = min(ATT_Q_TILE, s_p)
        attend_p = lambda q, k, v: _prompt_attention(
            q, k, v, lam_vecs, sub_g, lam_init=lam_init, tq=tq, tk=min(ATT_K_TILE, tq))
        yp, kp, vp, _, cp = _layer(yp, pos_p, zero_hist, attend_p, lw, gmlp_len=GROUP_DIM,
                                   transposed=True, emit_vn=False, n_streams=1,
                                   tm=min(ROW_TILE, s_p))

        ck = cache_k[l].reshape(b_s, past_len, ATT_W)
        cv = cache_v[l]
        attend_s = lambda q, k, v: _sample_attention(
            q, k, v, ck, cv, lam_vecs, sub_g, lam_init=lam_init, t_new=t_new)
        ys, ks, vs, gs, cs = _layer(ys, pos_s, state_conv[l], attend_s, lw, gmlp_len=t_new,
                                    transposed=False, emit_vn=True, n_streams=b_s,
                                    tm=b_s * t_new)

        for acc, val in zip(outs, (
                kp.reshape(b_p, s_p, N_HEADS, 2, HEAD_DIM), vp.reshape(b_p, s_p, N_HEADS, HEAD_W), cp,
                ks.reshape(b_s, t_new, N_HEADS, 2, HEAD_DIM), vs.reshape(b_s, t_new, N_HEADS, HEAD_W),
                gs.reshape(b_s, t_new, N_GROUPS, GROUP_DIM), cs)):
            acc.append(val)

    return (yp.reshape(b_p, s_p, d), ys.reshape(b_s, t_new, d)) + tuple(jnp.stack(o) for o in outs)
```

```python
import functools
import math

import jax
import jax.numpy as jnp
from jax import lax
from jax.experimental import pallas as pl
from jax.experimental.pallas import tpu as pltpu

F32 = jnp.float32
BF16 = jnp.bfloat16

N_HEADS = 8
HEAD_DIM = 64
HEAD_W = 2 * HEAD_DIM
CHUNK = 64
ROT_DIM = HEAD_DIM // 4
ROT_HALF = ROT_DIM // 2
ROPE_THETA = 500000.0
ATT_SCALE = HEAD_DIM ** -0.5
N_GROUPS = 8
GROUP_DIM = 128
GMLP_W = N_GROUPS * GROUP_DIM
ATT_W = N_HEADS * HEAD_W
CONV_W = 3
EPS = 1e-6
LOG2E = 1.4426950408889634

LANES = 128
SUBLANES = 8
VMEM_BYTES_V7X = 64 * 1024 * 1024

ROW_TILE = 512
PROJ_ROW_TILE = 256
FF_TILE = 512
FF_SUBTILE = 256
CONV_PHASES = 4
ATT_Q_TILE = 512
ATT_K_TILE = 512


def _vmem_limit(estimate_bytes):
    return int(min(VMEM_BYTES_V7X - (4 << 20), max(32 << 20, estimate_bytes)))


def _gelu(x):
    a = -2.0 * math.sqrt(2.0 / math.pi) * LOG2E
    return x / (1.0 + jnp.exp2(x * (a + (a * 0.044715) * (x * x))))


def _rms(x, g):
    return x * lax.rsqrt(jnp.mean(x * x, axis=-1, keepdims=True) + EPS) * g


def _rope(z, cos, sin_lo, sin_hi):
    outs = []
    for s in range(z.shape[1] // LANES):
        xs = z[:, s * LANES:(s + 1) * LANES]
        nxt = pltpu.roll(xs, LANES - ROT_HALF, axis=1)
        prv = pltpu.roll(xs, ROT_HALF, axis=1)
        outs.append(xs * cos + nxt * sin_lo + prv * sin_hi)
    return jnp.concatenate(outs, axis=1)


def _inproj_kernel(x_ref, gpre_ref, w_ref, gv_ref, ws_ref, bs_ref, cos_ref, slo_ref, shi_ref,
                   *refs, gmlp_len, transposed, emit_vn, k_tile):
    if emit_vn:
        outa_ref, q_ref, k32_ref, kbf_ref, v32_ref, vbf_ref, vn_ref = refs
    else:
        outa_ref, q_ref, k32_ref, kbf_ref, v32_ref, vbf_ref = refs
        vn_ref = None
    tm = x_ref.shape[0]
    hn = _rms(x_ref[...], gpre_ref[...]).astype(BF16)
    block = lambda n: jnp.dot(hn, w_ref[:, n * GMLP_W:(n + 1) * GMLP_W],
                              preferred_element_type=F32)

    u = _gelu(block(0))
    vn = _rms(_gelu(block(1)), gv_ref[...])
    if emit_vn:
        vn_ref[...] = vn
    vnb = vn.astype(BF16)
    ln = gmlp_len
    causal = (lax.broadcasted_iota(jnp.int32, (ln, ln), 1)
              <= lax.broadcasted_iota(jnp.int32, (ln, ln), 0))
    for g in range(N_GROUPS):
        wg = jnp.where(causal, ws_ref[g], 0.0).astype(BF16)
        cols = slice(g * GROUP_DIM, (g + 1) * GROUP_DIM)
        bias = bs_ref[:, cols]
        for ch in range(tm // ln):
            rows = slice(ch * ln, (ch + 1) * ln)
            sg = jnp.dot(wg, vnb[rows, cols], preferred_element_type=F32) + bias
            outa_ref[rows, cols] = (u[rows, cols] * sg).astype(BF16)

    cos, slo, shi = cos_ref[...], slo_ref[...], shi_ref[...]
    q = _rope(block(2), cos, slo, shi) * (ATT_SCALE * LOG2E)
    q_ref[...] = q.T.astype(BF16) if transposed else q.astype(BF16)

    k = _rope(block(3), cos, slo, shi)
    k32_ref[...] = k
    kbf_ref[...] = k.astype(BF16)

    v = block(4)
    v32_ref[...] = v
    if not transposed:
        vbf_ref[...] = v.astype(BF16)
    else:
        vt = v.T.astype(BF16)
        if k_tile >= tm:
            vbf_ref[0] = vt
        else:
            for cb in range(tm // k_tile):
                vbf_ref[cb] = vt[:, cb * k_tile:(cb + 1) * k_tile]


def _rope_tables(start, count):
    lane = jnp.arange(LANES)
    in_block = lane % HEAD_DIM
    inv_freq = ROPE_THETA ** (-jnp.arange(ROT_HALF, dtype=F32) / ROT_HALF)
    freq = inv_freq[lane % ROT_HALF][None, :]
    a0 = start // LANES
    n_hi = -(-(start + count) // LANES) - a0
    hi = ((a0 + jnp.arange(n_hi)) * LANES).astype(F32)[:, None] * freq
    lo = jnp.arange(LANES).astype(F32)[:, None] * freq
    cos_hi, sin_hi = jnp.cos(hi)[:, None, :], jnp.sin(hi)[:, None, :]
    cos_lo, sin_lo = jnp.cos(lo)[None], jnp.sin(lo)[None]
    first = start - a0 * LANES
    rows = lambda t: t.reshape(n_hi * LANES, LANES)[first:first + count]
    cos = rows(cos_hi * cos_lo - sin_hi * sin_lo)
    sin = rows(sin_hi * cos_lo + cos_hi * sin_lo)
    rot_lo = in_block < ROT_HALF
    rot_hi = (in_block >= ROT_HALF) & (in_block < ROT_DIM)
    return (jnp.where(rot_lo | rot_hi, cos, 1.0), jnp.where(rot_lo, -sin, 0.0),
            jnp.where(rot_hi, sin, 0.0))


def _inproj(x, pos, g_pre, w_in, gv, w_s, b_s, *, gmlp_len, transposed, emit_vn, tm):
    rows, d = x.shape
    assert rows % tm == 0 and tm % gmlp_len == 0
    k_tile = min(ATT_K_TILE, rows)
    assert k_tile % tm == 0 or tm % k_tile == 0
    start, count, repeats = pos
    assert count * repeats == rows
    cos, slo, shi = (jnp.tile(t, (repeats, 1)) for t in _rope_tables(start, count))
    ws = w_s[:, :gmlp_len, :gmlp_len]
    bs = jnp.repeat(b_s[:, :gmlp_len].T, GROUP_DIM, axis=1)

    row_blk = lambda w: pl.BlockSpec((tm, w), lambda i: (i, 0))
    const = lambda shape: pl.BlockSpec(shape, lambda i: (0,) * len(shape), pipeline_mode=pl.Buffered(1))
    in_specs = [
        row_blk(d), const((1, d)), const(w_in.shape),
        const((1, GMLP_W)), const(ws.shape), const(bs.shape),
        row_blk(LANES), row_blk(LANES), row_blk(LANES),
    ]
    if transposed:
        q_shape, q_spec = (ATT_W, rows), pl.BlockSpec((ATT_W, tm), lambda i: (0, i))
        v_shape = (rows // k_tile, ATT_W, k_tile)
        if k_tile >= tm:
            per = k_tile // tm
            v_spec = pl.BlockSpec((1, ATT_W, tm), lambda i: (i // per, 0, i % per))
        else:
            v_spec = pl.BlockSpec((tm // k_tile, ATT_W, k_tile), lambda i: (i, 0, 0))
    else:
        q_shape, q_spec = (rows, ATT_W), row_blk(ATT_W)
        v_shape, v_spec = (rows, ATT_W), row_blk(ATT_W)
    out_shape = [
        jax.ShapeDtypeStruct((rows, GMLP_W), BF16), jax.ShapeDtypeStruct(q_shape, BF16),
        jax.ShapeDtypeStruct((rows, ATT_W), F32), jax.ShapeDtypeStruct((rows, ATT_W), BF16),
        jax.ShapeDtypeStruct((rows, ATT_W), F32), jax.ShapeDtypeStruct(v_shape, BF16),
    ]
    out_specs = [row_blk(GMLP_W), q_spec, row_blk(ATT_W), row_blk(ATT_W), row_blk(ATT_W), v_spec]
    if emit_vn:
        out_shape.append(jax.ShapeDtypeStruct((rows, GMLP_W), F32))
        out_specs.append(row_blk(GMLP_W))
    est = (2 * tm * d * 4 + w_in.size * 2 + tm * d * 2
           + 2 * tm * GMLP_W * (2 + 2 + 4 + 2 + 4 + 2 + 4) + 12 * tm * GMLP_W * 4)
    return pl.pallas_call(
        functools.partial(_inproj_kernel, gmlp_len=gmlp_len, transposed=transposed,
                          emit_vn=emit_vn, k_tile=k_tile),
        grid=(rows // tm,),
        in_specs=in_specs, out_specs=out_specs, out_shape=out_shape,
        compiler_params=pltpu.CompilerParams(
            dimension_semantics=("arbitrary",), vmem_limit_bytes=_vmem_limit(est)),
        name="inproj",
    )(x, g_pre, w_in, gv, ws, bs, cos, slo, shi)


def _lambda(lq1_ref, lk1_ref, lq2_ref, lk2_ref, lam_init):
    d1 = jnp.sum(lq1_ref[...] * lk1_ref[...], axis=-1, keepdims=True)
    d2 = jnp.sum(lq2_ref[...] * lk2_ref[...], axis=-1, keepdims=True)
    return jnp.exp(d1) - jnp.exp(d2) + lam_init


def _prompt_attn_kernel(qt_ref, k_ref, vt_ref, lq1_ref, lk1_ref, lq2_ref, lk2_ref, g_ref, out_ref,
                        s_buf, smax_buf, p_buf, alpha_buf, m_s, l_s, acc_s, *, tq, tk, lam_init):
    qi = pl.program_id(1)
    qt = qt_ref[...]
    comp = lax.broadcasted_iota(jnp.int32, qt.shape, 0) < HEAD_DIM
    zero = jnp.zeros_like(qt)
    qs = (jnp.where(comp, qt, zero), jnp.where(comp, zero, qt))

    m_s[...] = jnp.full(m_s.shape, -jnp.inf, F32)
    l_s[...] = jnp.zeros(l_s.shape, F32)
    acc_s[...] = jnp.zeros(acc_s.shape, F32)

    steps_per_tile = tq // tk
    n_full = qi * steps_per_tile
    n_total = n_full + steps_per_tile

    def block_of(j):
        if isinstance(j, int):
            return n_full + j if j < steps_per_tile else j - steps_per_tile
        return jnp.where(j < steps_per_tile, n_full + j, j - steps_per_tile)

    def scores(j):
        kb = block_of(j)
        kblk = k_ref[pl.ds(pl.multiple_of(kb * tk, tk), tk), :]
        masked = isinstance(j, int) and j < steps_per_tile
        if masked:
            key_chunk = (lax.broadcasted_iota(jnp.int32, (tk, tq), 0) + j * tk) // CHUNK
            allowed = key_chunk <= lax.broadcasted_iota(jnp.int32, (tk, tq), 1) // CHUNK
        for c in range(2):
            s = jnp.dot(kblk, qs[c], preferred_element_type=F32)
            if masked:
                s = jnp.where(allowed, s, -jnp.inf)
            s_buf[c] = s
            smax_buf[c] = jnp.max(s, axis=0, keepdims=True)

    def softmax():
        for c in range(2):
            m_old = m_s[c]
            m_new = jnp.maximum(m_old, smax_buf[c])
            alpha = jnp.exp2(m_old - m_new)
            alpha_buf[c] = alpha
            p = jnp.exp2(s_buf[c] - m_new)
            l_s[c] = alpha * l_s[c] + jnp.sum(p, axis=0, keepdims=True)
            p_buf[c] = p.astype(BF16)
            m_s[c] = m_new

    def values(j):
        vtb = vt_ref[block_of(j)]
        for c in range(2):
            acc_s[c] = alpha_buf[c] * acc_s[c] + jnp.dot(vtb, p_buf[c], preferred_element_type=F32)

    def step(u, with_values=True, with_softmax=True):
        if with_values:
            values(u - 2)
        if with_softmax:
            softmax()
        scores(u)

    def run_steps(first, stop):
        count = stop - first

        def pair(v, carry):
            step(first + 2 * v)
            step(first + 2 * v + 1)
            return carry

        lax.fori_loop(0, count // 2, pair, 0)
        pl.when(count % 2 == 1)(lambda: step(stop - 1))

    def drain():
        values(n_total - 2)
        softmax()

    if steps_per_tile >= 2:
        for u in range(steps_per_tile):
            step(u, with_values=u >= 2, with_softmax=u >= 1)
        run_steps(steps_per_tile, n_total)
        drain()
        values(n_total - 1)
    else:
        @pl.when(qi > 0)
        def _():
            step(0, with_values=False, with_softmax=False)
            step(1, with_values=False)
            run_steps(2, n_total)
            drain()
            values(n_total - 1)

        @pl.when(qi == 0)
        def _():
            step(0, with_values=False, with_softmax=False)
            softmax()
            values(0)

    lam = _lambda(lq1_ref, lk1_ref, lq2_ref, lk2_ref, lam_init)
    o = acc_s[0] / l_s[0] - lam * (acc_s[1] / l_s[1])
    on = o * lax.rsqrt(jnp.mean(o * o, axis=0, keepdims=True) + EPS)
    out_ref[...] = (on.T * (g_ref[...] * (1.0 - lam_init))).astype(BF16)


def _prompt_attention(qt, kbf, vt, lam_vecs, subln_g, *, lam_init, tq, tk):
    rows = kbf.shape[0]
    assert rows % tq == 0 and tq % tk == 0 and tk % CHUNK == 0 and vt.shape[2] == tk
    vec = pl.BlockSpec((1, HEAD_DIM), lambda h, i: (0, 0))
    est = 2 * 2 * rows * HEAD_W * 2 + 2 * HEAD_W * tq * 4 + 12 * tk * tq * 4
    return pl.pallas_call(
        functools.partial(_prompt_attn_kernel, tq=tq, tk=tk, lam_init=lam_init),
        grid=(N_HEADS, rows // tq),
        in_specs=[
            pl.BlockSpec((HEAD_W, tq), lambda h, i: (h, i)),
            pl.BlockSpec((rows, HEAD_W), lambda h, i: (0, h)),
            pl.BlockSpec((rows // tk, HEAD_W, tk), lambda h, i: (0, h, 0)),
            vec, vec, vec, vec,
            pl.BlockSpec((1, HEAD_W), lambda h, i: (0, 0)),
        ],
        out_specs=pl.BlockSpec((tq, HEAD_W), lambda h, i: (i, h)),
        out_shape=jax.ShapeDtypeStruct((rows, ATT_W), BF16),
        scratch_shapes=[pltpu.VMEM((2, tk, tq), F32), pltpu.VMEM((2, 1, tq), F32),
                        pltpu.VMEM((2, tk, tq), BF16), pltpu.VMEM((2, 1, tq), F32),
                        pltpu.VMEM((2, 1, tq), F32), pltpu.VMEM((2, 1, tq), F32),
                        pltpu.VMEM((2, HEAD_W, tq), F32)],
        compiler_params=pltpu.CompilerParams(
            dimension_semantics=("arbitrary", "arbitrary"), vmem_limit_bytes=_vmem_limit(est)),
        name="prompt_attn",
    )(qt, kbf, vt, *lam_vecs, subln_g)


def _sample_attn_kernel(q_ref, kc_ref, vc_ref, kn_ref, vn_ref, lq1_ref, lk1_ref, lq2_ref, lk2_ref,
                        g_ref, out_ref, *, past_len, lam_init):
    t = q_ref.shape[0]
    comp = lax.broadcasted_iota(jnp.int32, (t, HEAD_W), 1) < HEAD_DIM
    q_chunk = (past_len + lax.broadcasted_iota(jnp.int32, (t, 1), 0)) // CHUNK
    ok_c = lax.broadcasted_iota(jnp.int32, (t, past_len), 1) // CHUNK <= q_chunk
    ok_n = (past_len + lax.broadcasted_iota(jnp.int32, (t, t), 1)) // CHUNK <= q_chunk
    nt = (((1,), (1,)), ((), ()))
    lam = _lambda(lq1_ref, lk1_ref, lq2_ref, lk2_ref, lam_init)
    for h in range(N_HEADS):
        cols = slice(h * HEAD_W, (h + 1) * HEAD_W)
        q = q_ref[:, cols]
        zero = jnp.zeros_like(q)
        qs = (jnp.where(comp, q, zero), jnp.where(comp, zero, q))
        kc = kc_ref[0, :, cols].astype(BF16)
        vc = vc_ref[0, :, h, :].astype(BF16)
        kn = kn_ref[:, cols]
        vn = vn_ref[:, cols]
        outs = []
        for c in range(2):
            sc = lax.dot_general(qs[c], kc, nt, preferred_element_type=F32)
            sn = lax.dot_general(qs[c], kn, nt, preferred_element_type=F32)
            sc = jnp.where(ok_c, sc, -jnp.inf)
            sn = jnp.where(ok_n, sn, -jnp.inf)
            m = jnp.maximum(jnp.max(sc, axis=-1, keepdims=True), jnp.max(sn, axis=-1, keepdims=True))
            pc = jnp.exp2(sc - m)
            pn = jnp.exp2(sn - m)
            den = jnp.sum(pc, axis=-1, keepdims=True) + jnp.sum(pn, axis=-1, keepdims=True)
            num = (jnp.dot(pc.astype(BF16), vc, preferred_element_type=F32)
                   + jnp.dot(pn.astype(BF16), vn, preferred_element_type=F32))
            outs.append(num / den)
        o = outs[0] - lam * outs[1]
        out_ref[:, cols] = (_rms(o, g_ref[...]) * (1.0 - lam_init)).astype(BF16)


def _sample_attention(q, kbf, vbf, cache_k, cache_v, lam_vecs, subln_g, *, lam_init, t_new):
    n_streams, past_len = cache_k.shape[:2]
    vec = pl.BlockSpec((1, HEAD_DIM), lambda b: (0, 0))
    new_blk = pl.BlockSpec((t_new, ATT_W), lambda b: (b, 0))
    est = 2 * 2 * past_len * ATT_W * 4 + 16 * t_new * past_len * 4
    return pl.pallas_call(
        functools.partial(_sample_attn_kernel, past_len=past_len, lam_init=lam_init),
        grid=(n_streams,),
        in_specs=[new_blk, pl.BlockSpec((1, past_len, ATT_W), lambda b: (b, 0, 0)),
                  pl.BlockSpec((1, past_len, N_HEADS, HEAD_W), lambda b: (b, 0, 0, 0)),
                  new_blk, new_blk, vec, vec, vec, vec,
                  pl.BlockSpec((1, HEAD_W), lambda b: (0, 0))],
        out_specs=new_blk,
        out_shape=jax.ShapeDtypeStruct((n_streams * t_new, ATT_W), BF16),
        compiler_params=pltpu.CompilerParams(
            dimension_semantics=("arbitrary",), vmem_limit_bytes=_vmem_limit(est)),
        name="sample_attn",
    )(q, cache_k, cache_v, kbf, vbf, *lam_vecs, subln_g)


def _outproj_kernel(x_ref, a_ref, b_ref, w_ref, gpost_ref, gffn_ref, x1_ref, hn_ref):
    mix = (jnp.dot(a_ref[...], w_ref[:GMLP_W, :], preferred_element_type=F32)
           + jnp.dot(b_ref[...], w_ref[GMLP_W:, :], preferred_element_type=F32))
    x1 = x_ref[...] + _rms(mix, gpost_ref[...])
    x1_ref[...] = x1
    hn_ref[...] = _rms(x1, gffn_ref[...]).astype(BF16)


def _outproj(x, out_a, out_b, w_out, g_post, g_ffn, *, tm):
    rows, d = x.shape
    row_blk = lambda w: pl.BlockSpec((tm, w), lambda i: (i, 0))
    const = lambda shape: pl.BlockSpec(shape, lambda i: (0,) * len(shape))
    est = 2 * tm * d * (4 + 4 + 2) + 2 * 2 * tm * GMLP_W * 2 + 2 * w_out.size * 2 + 4 * tm * d * 4
    return pl.pallas_call(
        _outproj_kernel,
        grid=(rows // tm,),
        in_specs=[row_blk(d), row_blk(GMLP_W), row_blk(ATT_W), const(w_out.shape),
                  const((1, d)), const((1, d))],
        out_specs=[row_blk(d), row_blk(d)],
        out_shape=[jax.ShapeDtypeStruct((rows, d), F32), jax.ShapeDtypeStruct((rows, d), BF16)],
        compiler_params=pltpu.CompilerParams(
            dimension_semantics=("arbitrary",), vmem_limit_bytes=_vmem_limit(est)),
        name="outproj",
    )(x, out_a, out_b, w_out, g_post, g_ffn)


def _ffn_kernel(hn_ref, x1_ref, wg_ref, wv_ref, cwg_ref, cwv_ref, cbg_ref, cbv_ref, hg_ref, hv_ref,
                wd_ref, gpost_ref, y_ref, cg_ref, cv_ref, acc_s, carg_s, carv_s, stage_s, order_s,
                *, n_streams, stream_len):
    i = pl.program_id(0)
    j = pl.program_id(1)
    tm = hn_ref.shape[0]
    ft = wg_ref.shape[1]
    hn = hn_ref[...]
    sub = min(FF_SUBTILE, ft)
    slabs = sub // LANES
    per_phase = stream_len // CONV_PHASES

    @pl.when(j == 0)
    def _():
        acc_s[...] = jnp.zeros(acc_s.shape, F32)

    if n_streams == 1:
        @pl.when(i == 0)
        def _():
            pad = jnp.zeros((SUBLANES - (CONV_W - 1), ft), F32)
            carg_s[j] = jnp.concatenate([pad, hg_ref[0]], axis=0)
            carv_s[j] = jnp.concatenate([pad, hv_ref[0]], axis=0)

    def conv_branch(buf, cs, w_ref, cw_ref, cb_ref, h_ref, car_s, cout_ref):
        up = jnp.dot(hn, w_ref[:, cs], preferred_element_type=F32)
        cw = cw_ref[:, cs]
        cb = cb_ref[:, cs]
        hist = car_s[j][:, cs] if n_streams == 1 else None
        halves = []
        for half in range(slabs):
            slab = buf * slabs + half
            lanes = slice(half * LANES, (half + 1) * LANES)
            out_lanes = slice(cs.start + half * LANES, cs.start + (half + 1) * LANES)
            for s in range(n_streams):
                if n_streams == 1:
                    stage_s[slab, s, 0:SUBLANES] = hist[:, lanes]
                else:
                    stage_s[slab, s, SUBLANES - (CONV_W - 1):SUBLANES] = h_ref[s, :, out_lanes]
                stage_s[slab, s, SUBLANES:SUBLANES + stream_len] = (
                    up[s * stream_len:(s + 1) * stream_len, lanes])
            phases = []
            for ph in range(CONV_PHASES):
                y = cb[:, lanes]
                for k in range(CONV_W):
                    first = SUBLANES - (CONV_W - 1) + k + ph
                    rows = [stage_s[slab, s, pl.ds(first, per_phase, stride=CONV_PHASES), :]
                            for s in range(n_streams)]
                    tap = rows[0] if n_streams == 1 else jnp.concatenate(rows, axis=0)
                    y = y + tap * cw[k:k + 1, lanes]
                phases.append(y)
            halves.append(jnp.concatenate(phases, axis=0))
            for s in range(n_streams):
                cout_ref[0, s, :, out_lanes] = stage_s[slab, s, SUBLANES + stream_len - (CONV_W - 1):
                                                       SUBLANES + stream_len]
            if n_streams == 1:
                car_s[j, :, out_lanes] = stage_s[slab, 0, stream_len:stream_len + SUBLANES]
        return jnp.concatenate(halves, axis=1)

    acts = []
    for c0 in range(0, ft, sub):
        cs = slice(c0, c0 + sub)
        buf = 2 * (c0 // sub)
        gate = conv_branch(buf, cs, wg_ref, cwg_ref, cbg_ref, hg_ref, carg_s, cg_ref)
        val = conv_branch(buf + 1, cs, wv_ref, cwv_ref, cbv_ref, hv_ref, carv_s, cv_ref)
        acts.append((_gelu(gate) * val).astype(BF16))
    acc_s[...] += jnp.dot(jnp.concatenate(acts, axis=1), wd_ref[...], preferred_element_type=F32)

    @pl.when(j == pl.num_programs(1) - 1)
    def _():
        fn = _rms(acc_s[...], gpost_ref[...])
        rows_per_phase = tm // CONV_PHASES
        for c in range(fn.shape[1] // LANES):
            for ph in range(CONV_PHASES):
                order_s[c, pl.ds(ph, rows_per_phase, stride=CONV_PHASES), :] = (
                    fn[ph * rows_per_phase:(ph + 1) * rows_per_phase, c * LANES:(c + 1) * LANES])
        f_nat = jnp.concatenate([order_s[c] for c in range(fn.shape[1] // LANES)], axis=1)
        y_ref[...] = x1_ref[...] + f_nat


def _ffn(hn, x1, w_up, conv_w, conv_b, hist, w_down, g_post, *, tm, ft, n_streams):
    rows, d = x1.shape
    d_ff = w_down.shape[0]
    nj = d_ff // ft
    n_tiles = rows // tm
    stream_len = tm // n_streams
    sub = min(FF_SUBTILE, ft)
    assert rows % tm == 0 and d_ff % ft == 0 and (n_streams == 1 or n_tiles == 1)
    assert hist.shape == (n_streams, CONV_W - 1, 2 * d_ff)
    assert stream_len % (SUBLANES * CONV_PHASES) == 0
    gate_col = lambda i, j: (0, j)
    val_col = lambda i, j: (0, nj + j)
    row_blk = lambda w: pl.BlockSpec((tm, w), lambda i, j: (i, 0))
    once_per_tile = lambda w: pl.BlockSpec((tm, w), lambda i, j: (i, 0), pipeline_mode=pl.Buffered(1))
    hist_blk = lambda off: pl.BlockSpec((n_streams, CONV_W - 1, ft), lambda i, j: (0, 0, off + j))
    tail_blk = pl.BlockSpec((1, n_streams, CONV_W - 1, ft), lambda i, j: (i, 0, 0, j))
    tail_shape = jax.ShapeDtypeStruct((n_tiles, n_streams, CONV_W - 1, d_ff), F32)
    est = (tm * d * (2 + 4 + 2 * 4) + tm * d * 4 + 2 * 3 * d * ft * 2 + 16 * tm * FF_SUBTILE * 4
           + 2 * tm * d * 4)
    y, cg, cv = pl.pallas_call(
        functools.partial(_ffn_kernel, n_streams=n_streams, stream_len=stream_len),
        grid=(n_tiles, nj),
        in_specs=[
            once_per_tile(d), once_per_tile(d),
            pl.BlockSpec((d, ft), gate_col), pl.BlockSpec((d, ft), val_col),
            pl.BlockSpec((CONV_W, ft), gate_col), pl.BlockSpec((CONV_W, ft), val_col),
            pl.BlockSpec((1, ft), gate_col), pl.BlockSpec((1, ft), val_col),
            hist_blk(0), hist_blk(nj),
            pl.BlockSpec((ft, d), lambda i, j: (j, 0)),
            pl.BlockSpec((1, d), lambda i, j: (0, 0)),
        ],
        out_specs=[row_blk(d), tail_blk, tail_blk],
        out_shape=[jax.ShapeDtypeStruct((rows, d), F32), tail_shape, tail_shape],
        scratch_shapes=[pltpu.VMEM((tm, d), F32),
                        pltpu.VMEM((nj, SUBLANES, ft), F32), pltpu.VMEM((nj, SUBLANES, ft), F32),
                        pltpu.VMEM((2 * (ft // sub) * (sub // LANES), n_streams,
                                    SUBLANES + stream_len, LANES), F32),
                        pltpu.VMEM((d // LANES, tm, LANES), F32)],
        compiler_params=pltpu.CompilerParams(
            dimension_semantics=("arbitrary", "arbitrary"), vmem_limit_bytes=_vmem_limit(est)),
        name="convffn",
    )(hn, x1, w_up, w_up, conv_w, conv_w, conv_b, conv_b, hist, hist, w_down, g_post)
    return y, jnp.concatenate([cg[-1], cv[-1]], axis=-1)


def _layer(x, pos, hist, attend, lw, *, gmlp_len, transposed, emit_vn, n_streams, tm):
    (g_mix_pre, w_in, gv, w_s, b_s, subln_g, w_out, g_mix_post, g_ffn_pre, w_up, conv_w, conv_b,
     w_down, g_ffn_post) = lw
    res = _inproj(x, pos, g_mix_pre, w_in, gv, w_s, b_s, gmlp_len=gmlp_len, transposed=transposed,
                  emit_vn=emit_vn, tm=min(PROJ_ROW_TILE, tm))
    out_a, q, k32, kbf, v32, vbf = res[:6]
    vn = res[6] if emit_vn else None
    out_b = attend(q, kbf, vbf)
    x1, hn2 = _outproj(x, out_a, out_b, w_out, g_mix_post, g_ffn_pre, tm=tm)
    y, new_hist = _ffn(hn2, x1, w_up, conv_w, conv_b, hist, w_down, g_ffn_post,
                       tm=tm, ft=FF_TILE, n_streams=n_streams)
    return y, k32, v32, vn, new_hist


def kernel(x_prompt, x_sample, cache_k, cache_v, state_conv, g_mix_pre, w_in, gmlp_v_gain, gmlp_w_s,
           gmlp_b_s, lambda_q1, lambda_k1, lambda_q2, lambda_k2, subln_g, w_out, g_mix_post,
           g_ffn_pre, w_up, conv_w, conv_b, w_down, g_ffn_post):
    b_p, s_p, d = x_prompt.shape
    b_s, t_new, _ = x_sample.shape
    depth = w_in.shape[0]
    past_len = cache_k.shape[2]
    d_ff = w_down.shape[1]
    assert b_p == 1
    pos_p = (0, s_p, 1)
    pos_s = (past_len, t_new, b_s)
    zero_hist = jnp.zeros((b_p, CONV_W - 1, 2 * d_ff), F32)
    row = lambda a: a.reshape(1, -1)

    yp = x_prompt.reshape(b_p * s_p, d)
    ys = x_sample.reshape(b_s * t_new, d)
    outs = [[] for _ in range(7)]
    for l in range(depth):
        lam_init = 0.8 - 0.6 * math.exp(-0.3 * l)
        lam_vecs = tuple(row(a[l]) for a in (lambda_q1, lambda_k1, lambda_q2, lambda_k2))
        sub_g = row(subln_g[l])
        lw = (row(g_mix_pre[l]), w_in[l].astype(BF16), row(gmlp_v_gain[l]), gmlp_w_s[l], gmlp_b_s[l],
              sub_g, w_out[l].astype(BF16), row(g_mix_post[l]), row(g_ffn_pre[l]),
              w_up[l].astype(BF16), conv_w[l], row(conv_b[l]), w_down[l].astype(BF16),
              row(g_ffn_post[l]))

        tq = min(ATT_Q_TILE, s_p)
        attend_p = lambda q, k, v: _prompt_attention(
            q, k, v, lam_vecs, sub_g, lam_init=lam_init, tq=tq, tk=min(ATT_K_TILE, tq))
        yp, kp, vp, _, cp = _layer(yp, pos_p, zero_hist, attend_p, lw, gmlp_len=GROUP_DIM,
                                   transposed=True, emit_vn=False, n_streams=1,
                                   tm=min(ROW_TILE, s_p))

        ck = cache_k[l].reshape(b_s, past_len, ATT_W)
        cv = cache_v[l]
        attend_s = lambda q, k, v: _sample_attention(
            q, k, v, ck, cv, lam_vecs, sub_g, lam_init=lam_init, t_new=t_new)
        ys, ks, vs, gs, cs = _layer(ys, pos_s, state_conv[l], attend_s, lw, gmlp_len=t_new,
                                    transposed=False, emit_vn=True, n_streams=b_s,
                                    tm=b_s * t_new)

        for acc, val in zip(outs, (
                kp.reshape(b_p, s_p, N_HEADS, 2, HEAD_DIM), vp.reshape(b_p, s_p, N_HEADS, HEAD_W), cp,
                ks.reshape(b_s, t_new, N_HEADS, 2, HEAD_DIM), vs.reshape(b_s, t_new, N_HEADS, HEAD_W),
                gs.reshape(b_s, t_new, N_GROUPS, GROUP_DIM), cs)):
            acc.append(val)

    return (yp.reshape(b_p, s_p, d), ys.reshape(b_s, t_new, d)) + tuple(jnp.stack(o) for o in outs)
```

```python
import functools
import math

import jax
import jax.numpy as jnp
from jax import lax
from jax.experimental import pallas as pl
from jax.experimental.pallas import tpu as pltpu

F32 = jnp.float32
BF16 = jnp.bfloat16

N_HEADS = 8
HEAD_DIM = 64
HEAD_W = 2 * HEAD_DIM
CHUNK = 64
ROT_DIM = HEAD_DIM // 4
ROT_HALF = ROT_DIM // 2
ROPE_THETA = 500000.0
ATT_SCALE = HEAD_DIM ** -0.5
N_GROUPS = 8
GROUP_DIM = 128
GMLP_W = N_GROUPS * GROUP_DIM
ATT_W = N_HEADS * HEAD_W
CONV_W = 3
EPS = 1e-6
LOG2E = 1.4426950408889634

LANES = 128
SUBLANES = 8
VMEM_BYTES_V7X = 64 * 1024 * 1024

ROW_TILE = 512
PROJ_ROW_TILE = 512
FF_TILE = 512
FF_SUBTILE = 256
CONV_PHASES = 4
ATT_Q_TILE = 512
ATT_K_TILE = 512


def _vmem_limit(estimate_bytes):
    return int(min(VMEM_BYTES_V7X - (4 << 20), max(32 << 20, estimate_bytes)))


def _gelu(x):
    a = -2.0 * math.sqrt(2.0 / math.pi) * LOG2E
    return x / (1.0 + jnp.exp2(x * (a + (a * 0.044715) * (x * x))))


def _rms(x, g):
    return x * lax.rsqrt(jnp.mean(x * x, axis=-1, keepdims=True) + EPS) * g


def _rope(z, cos, sin_lo, sin_hi):
    outs = []
    for s in range(z.shape[1] // LANES):
        xs = z[:, s * LANES:(s + 1) * LANES]
        nxt = pltpu.roll(xs, LANES - ROT_HALF, axis=1)
        prv = pltpu.roll(xs, ROT_HALF, axis=1)
        outs.append(xs * cos + nxt * sin_lo + prv * sin_hi)
    return jnp.concatenate(outs, axis=1)


def _inproj_kernel(x_ref, gpre_ref, w_ref, gv_ref, ws_ref, bs_ref, cos_ref, slo_ref, shi_ref,
                   *refs, gmlp_len, transposed, emit_vn, k_tile):
    if emit_vn:
        outa_ref, q_ref, k32_ref, kbf_ref, v32_ref, vbf_ref, vn_ref = refs
    else:
        outa_ref, q_ref, k32_ref, kbf_ref, v32_ref, vbf_ref = refs
        vn_ref = None
    tm = x_ref.shape[0]
    hn = _rms(x_ref[...], gpre_ref[...]).astype(BF16)
    block = lambda n: jnp.dot(hn, w_ref[:, n * GMLP_W:(n + 1) * GMLP_W],
                              preferred_element_type=F32)

    u = _gelu(block(0))
    vn = _rms(_gelu(block(1)), gv_ref[...])
    if emit_vn:
        vn_ref[...] = vn
    vnb = vn.astype(BF16)
    ln = gmlp_len
    causal = (lax.broadcasted_iota(jnp.int32, (ln, ln), 1)
              <= lax.broadcasted_iota(jnp.int32, (ln, ln), 0))
    for g in range(N_GROUPS):
        wg = jnp.where(causal, ws_ref[g], 0.0).astype(BF16)
        cols = slice(g * GROUP_DIM, (g + 1) * GROUP_DIM)
        bias = bs_ref[:, cols]
        for ch in range(tm // ln):
            rows = slice(ch * ln, (ch + 1) * ln)
            sg = jnp.dot(wg, vnb[rows, cols], preferred_element_type=F32) + bias
            outa_ref[rows, cols] = (u[rows, cols] * sg).astype(BF16)

    cos, slo, shi = cos_ref[...], slo_ref[...], shi_ref[...]
    q = _rope(block(2), cos, slo, shi) * (ATT_SCALE * LOG2E)
    q_ref[...] = q.T.astype(BF16) if transposed else q.astype(BF16)

    k = _rope(block(3), cos, slo, shi)
    k32_ref[...] = k
    kbf_ref[...] = k.astype(BF16)

    v = block(4)
    v32_ref[...] = v
    if not transposed:
        vbf_ref[...] = v.astype(BF16)
    else:
        vt = v.T.astype(BF16)
        if k_tile >= tm:
            vbf_ref[0] = vt
        else:
            for cb in range(tm // k_tile):
                vbf_ref[cb] = vt[:, cb * k_tile:(cb + 1) * k_tile]


def _rope_tables(start, count):
    lane = jnp.arange(LANES)
    in_block = lane % HEAD_DIM
    inv_freq = ROPE_THETA ** (-jnp.arange(ROT_HALF, dtype=F32) / ROT_HALF)
    freq = inv_freq[lane % ROT_HALF][None, :]
    a0 = start // LANES
    n_hi = -(-(start + count) // LANES) - a0
    hi = ((a0 + jnp.arange(n_hi)) * LANES).astype(F32)[:, None] * freq
    lo = jnp.arange(LANES).astype(F32)[:, None] * freq
    cos_hi, sin_hi = jnp.cos(hi)[:, None, :], jnp.sin(hi)[:, None, :]
    cos_lo, sin_lo = jnp.cos(lo)[None], jnp.sin(lo)[None]
    first = start - a0 * LANES
    rows = lambda t: t.reshape(n_hi * LANES, LANES)[first:first + count]
    cos = rows(cos_hi * cos_lo - sin_hi * sin_lo)
    sin = rows(sin_hi * cos_lo + cos_hi * sin_lo)
    rot_lo = in_block < ROT_HALF
    rot_hi = (in_block >= ROT_HALF) & (in_block < ROT_DIM)
    return (jnp.where(rot_lo | rot_hi, cos, 1.0), jnp.where(rot_lo, -sin, 0.0),
            jnp.where(rot_hi, sin, 0.0))


def _inproj(x, pos, g_pre, w_in, gv, w_s, b_s, *, gmlp_len, transposed, emit_vn, tm):
    rows, d = x.shape
    assert rows % tm == 0 and tm % gmlp_len == 0
    k_tile = min(ATT_K_TILE, rows)
    assert k_tile % tm == 0 or tm % k_tile == 0
    start, count, repeats = pos
    assert count * repeats == rows
    cos, slo, shi = (jnp.tile(t, (repeats, 1)) for t in _rope_tables(start, count))
    ws = w_s[:, :gmlp_len, :gmlp_len]
    bs = jnp.repeat(b_s[:, :gmlp_len].T, GROUP_DIM, axis=1)

    row_blk = lambda w: pl.BlockSpec((tm, w), lambda i: (i, 0))
    const = lambda shape: pl.BlockSpec(shape, lambda i: (0,) * len(shape), pipeline_mode=pl.Buffered(1))
    in_specs = [
        row_blk(d), const((1, d)), const(w_in.shape),
        const((1, GMLP_W)), const(ws.shape), const(bs.shape),
        row_blk(LANES), row_blk(LANES), row_blk(LANES),
    ]
    if transposed:
        q_shape, q_spec = (ATT_W, rows), pl.BlockSpec((ATT_W, tm), lambda i: (0, i))
        v_shape = (rows // k_tile, ATT_W, k_tile)
        if k_tile >= tm:
            per = k_tile // tm
            v_spec = pl.BlockSpec((1, ATT_W, tm), lambda i: (i // per, 0, i % per))
        else:
            v_spec = pl.BlockSpec((tm // k_tile, ATT_W, k_tile), lambda i: (i, 0, 0))
    else:
        q_shape, q_spec = (rows, ATT_W), row_blk(ATT_W)
        v_shape, v_spec = (rows, ATT_W), row_blk(ATT_W)
    out_shape = [
        jax.ShapeDtypeStruct((rows, GMLP_W), BF16), jax.ShapeDtypeStruct(q_shape, BF16),
        jax.ShapeDtypeStruct((rows, ATT_W), F32), jax.ShapeDtypeStruct((rows, ATT_W), BF16),
        jax.ShapeDtypeStruct((rows, ATT_W), F32), jax.ShapeDtypeStruct(v_shape, BF16),
    ]
    out_specs = [row_blk(GMLP_W), q_spec, row_blk(ATT_W), row_blk(ATT_W), row_blk(ATT_W), v_spec]
    if emit_vn:
        out_shape.append(jax.ShapeDtypeStruct((rows, GMLP_W), F32))
        out_specs.append(row_blk(GMLP_W))
    est = (2 * tm * d * 4 + w_in.size * 2 + tm * d * 2
           + 2 * tm * GMLP_W * (2 + 2 + 4 + 2 + 4 + 2 + 4) + 12 * tm * GMLP_W * 4)
    return pl.pallas_call(
        functools.partial(_inproj_kernel, gmlp_len=gmlp_len, transposed=transposed,
                          emit_vn=emit_vn, k_tile=k_tile),
        grid=(rows // tm,),
        in_specs=in_specs, out_specs=out_specs, out_shape=out_shape,
        compiler_params=pltpu.CompilerParams(
            dimension_semantics=("arbitrary",), vmem_limit_bytes=_vmem_limit(est)),
        name="inproj",
    )(x, g_pre, w_in, gv, ws, bs, cos, slo, shi)


def _lambda(lq1_ref, lk1_ref, lq2_ref, lk2_ref, lam_init):
    d1 = jnp.sum(lq1_ref[...] * lk1_ref[...], axis=-1, keepdims=True)
    d2 = jnp.sum(lq2_ref[...] * lk2_ref[...], axis=-1, keepdims=True)
    return jnp.exp(d1) - jnp.exp(d2) + lam_init


def _prompt_attn_kernel(qt_ref, k_ref, vt_ref, lq1_ref, lk1_ref, lq2_ref, lk2_ref, g_ref, out_ref,
                        s_buf, smax_buf, p_buf, alpha_buf, m_s, l_s, acc_s, *, tq, tk, lam_init):
    qi = pl.program_id(1)
    qt = qt_ref[...]
    comp = lax.broadcasted_iota(jnp.int32, qt.shape, 0) < HEAD_DIM
    zero = jnp.zeros_like(qt)
    qs = (jnp.where(comp, qt, zero), jnp.where(comp, zero, qt))

    m_s[...] = jnp.full(m_s.shape, -jnp.inf, F32)
    l_s[...] = jnp.zeros(l_s.shape, F32)
    acc_s[...] = jnp.zeros(acc_s.shape, F32)

    steps_per_tile = tq // tk
    n_full = qi * steps_per_tile
    n_total = n_full + steps_per_tile

    def block_of(j):
        if isinstance(j, int):
            return n_full + j if j < steps_per_tile else j - steps_per_tile
        return jnp.where(j < steps_per_tile, n_full + j, j - steps_per_tile)

    def scores(j):
        kb = block_of(j)
        kblk = k_ref[pl.ds(pl.multiple_of(kb * tk, tk), tk), :]
        masked = isinstance(j, int) and j < steps_per_tile
        if masked:
            key_chunk = (lax.broadcasted_iota(jnp.int32, (tk, tq), 0) + j * tk) // CHUNK
            allowed = key_chunk <= lax.broadcasted_iota(jnp.int32, (tk, tq), 1) // CHUNK
        for c in range(2):
            s = jnp.dot(kblk, qs[c], preferred_element_type=F32)
            if masked:
                s = jnp.where(allowed, s, -jnp.inf)
            s_buf[c] = s
            smax_buf[c] = jnp.max(s, axis=0, keepdims=True)

    def softmax():
        for c in range(2):
            m_old = m_s[c]
            m_new = jnp.maximum(m_old, smax_buf[c])
            alpha = jnp.exp2(m_old - m_new)
            alpha_buf[c] = alpha
            p = jnp.exp2(s_buf[c] - m_new)
            l_s[c] = alpha * l_s[c] + jnp.sum(p, axis=0, keepdims=True)
            p_buf[c] = p.astype(BF16)
            m_s[c] = m_new

    def values(j):
        vtb = vt_ref[block_of(j)]
        for c in range(2):
            acc_s[c] = alpha_buf[c] * acc_s[c] + jnp.dot(vtb, p_buf[c], preferred_element_type=F32)

    def step(u, with_values=True, with_softmax=True):
        if with_values:
            values(u - 2)
        if with_softmax:
            softmax()
        scores(u)

    def run_steps(first, stop):
        count = stop - first

        def pair(v, carry):
            step(first + 2 * v)
            step(first + 2 * v + 1)
            return carry

        lax.fori_loop(0, count // 2, pair, 0)
        pl.when(count % 2 == 1)(lambda: step(stop - 1))

    def drain():
        values(n_total - 2)
        softmax()

    if steps_per_tile >= 2:
        for u in range(steps_per_tile):
            step(u, with_values=u >= 2, with_softmax=u >= 1)
        run_steps(steps_per_tile, n_total)
        drain()
        values(n_total - 1)
    else:
        @pl.when(qi > 0)
        def _():
            step(0, with_values=False, with_softmax=False)
            step(1, with_values=False)
            run_steps(2, n_total)
            drain()
            values(n_total - 1)

        @pl.when(qi == 0)
        def _():
            step(0, with_values=False, with_softmax=False)
            softmax()
            values(0)

    lam = _lambda(lq1_ref, lk1_ref, lq2_ref, lk2_ref, lam_init)
    o = acc_s[0] / l_s[0] - lam * (acc_s[1] / l_s[1])
    on = o * lax.rsqrt(jnp.mean(o * o, axis=0, keepdims=True) + EPS)
    out_ref[...] = (on.T * (g_ref[...] * (1.0 - lam_init))).astype(BF16)


def _prompt_attention(qt, kbf, vt, lam_vecs, subln_g, *, lam_init, tq, tk):
    rows = kbf.shape[0]
    assert rows % tq == 0 and tq % tk == 0 and tk % CHUNK == 0 and vt.shape[2] == tk
    vec = pl.BlockSpec((1, HEAD_DIM), lambda h, i: (0, 0))
    est = 2 * 2 * rows * HEAD_W * 2 + 2 * HEAD_W * tq * 4 + 12 * tk * tq * 4
    return pl.pallas_call(
        functools.partial(_prompt_attn_kernel, tq=tq, tk=tk, lam_init=lam_init),
        grid=(N_HEADS, rows // tq),
        in_specs=[
            pl.BlockSpec((HEAD_W, tq), lambda h, i: (h, i)),
            pl.BlockSpec((rows, HEAD_W), lambda h, i: (0, h)),
            pl.BlockSpec((rows // tk, HEAD_W, tk), lambda h, i: (0, h, 0)),
            vec, vec, vec, vec,
            pl.BlockSpec((1, HEAD_W), lambda h, i: (0, 0)),
        ],
        out_specs=pl.BlockSpec((tq, HEAD_W), lambda h, i: (i, h)),
        out_shape=jax.ShapeDtypeStruct((rows, ATT_W), BF16),
        scratch_shapes=[pltpu.VMEM((2, tk, tq), F32), pltpu.VMEM((2, 1, tq), F32),
                        pltpu.VMEM((2, tk, tq), BF16), pltpu.VMEM((2, 1, tq), F32),
                        pltpu.VMEM((2, 1, tq), F32), pltpu.VMEM((2, 1, tq), F32),
                        pltpu.VMEM((2, HEAD_W, tq), F32)],
        compiler_params=pltpu.CompilerParams(
            dimension_semantics=("arbitrary", "arbitrary"), vmem_limit_bytes=_vmem_limit(est)),
        name="prompt_attn",
    )(qt, kbf, vt, *lam_vecs, subln_g)


def _sample_attn_kernel(q_ref, kc_ref, vc_ref, kn_ref, vn_ref, lq1_ref, lk1_ref, lq2_ref, lk2_ref,
                        g_ref, out_ref, *, past_len, lam_init):
    t = q_ref.shape[0]
    comp = lax.broadcasted_iota(jnp.int32, (t, HEAD_W), 1) < HEAD_DIM
    q_chunk = (past_len + lax.broadcasted_iota(jnp.int32, (t, 1), 0)) // CHUNK
    ok_c = lax.broadcasted_iota(jnp.int32, (t, past_len), 1) // CHUNK <= q_chunk
    ok_n = (past_len + lax.broadcasted_iota(jnp.int32, (t, t), 1)) // CHUNK <= q_chunk
    nt = (((1,), (1,)), ((), ()))
    lam = _lambda(lq1_ref, lk1_ref, lq2_ref, lk2_ref, lam_init)
    for h in range(N_HEADS):
        cols = slice(h * HEAD_W, (h + 1) * HEAD_W)
        q = q_ref[:, cols]
        zero = jnp.zeros_like(q)
        qs = (jnp.where(comp, q, zero), jnp.where(comp, zero, q))
        kc = kc_ref[0, :, cols].astype(BF16)
        vc = vc_ref[0, :, h, :].astype(BF16)
        kn = kn_ref[:, cols]
        vn = vn_ref[:, cols]
        outs = []
        for c in range(2):
            sc = lax.dot_general(qs[c], kc, nt, preferred_element_type=F32)
            sn = lax.dot_general(qs[c], kn, nt, preferred_element_type=F32)
            sc = jnp.where(ok_c, sc, -jnp.inf)
            sn = jnp.where(ok_n, sn, -jnp.inf)
            m = jnp.maximum(jnp.max(sc, axis=-1, keepdims=True), jnp.max(sn, axis=-1, keepdims=True))
            pc = jnp.exp2(sc - m)
            pn = jnp.exp2(sn - m)
            den = jnp.sum(pc, axis=-1, keepdims=True) + jnp.sum(pn, axis=-1, keepdims=True)
            num = (jnp.dot(pc.astype(BF16), vc, preferred_element_type=F32)
                   + jnp.dot(pn.astype(BF16), vn, preferred_element_type=F32))
            outs.append(num / den)
        o = outs[0] - lam * outs[1]
        out_ref[:, cols] = (_rms(o, g_ref[...]) * (1.0 - lam_init)).astype(BF16)


def _sample_attention(q, kbf, vbf, cache_k, cache_v, lam_vecs, subln_g, *, lam_init, t_new):
    n_streams, past_len = cache_k.shape[:2]
    vec = pl.BlockSpec((1, HEAD_DIM), lambda b: (0, 0))
    new_blk = pl.BlockSpec((t_new, ATT_W), lambda b: (b, 0))
    est = 2 * 2 * past_len * ATT_W * 4 + 16 * t_new * past_len * 4
    return pl.pallas_call(
        functools.partial(_sample_attn_kernel, past_len=past_len, lam_init=lam_init),
        grid=(n_streams,),
        in_specs=[new_blk, pl.BlockSpec((1, past_len, ATT_W), lambda b: (b, 0, 0)),
                  pl.BlockSpec((1, past_len, N_HEADS, HEAD_W), lambda b: (b, 0, 0, 0)),
                  new_blk, new_blk, vec, vec, vec, vec,
                  pl.BlockSpec((1, HEAD_W), lambda b: (0, 0))],
        out_specs=new_blk,
        out_shape=jax.ShapeDtypeStruct((n_streams * t_new, ATT_W), BF16),
        compiler_params=pltpu.CompilerParams(
            dimension_semantics=("arbitrary",), vmem_limit_bytes=_vmem_limit(est)),
        name="sample_attn",
    )(q, cache_k, cache_v, kbf, vbf, *lam_vecs, subln_g)


def _outproj_kernel(x_ref, a_ref, b_ref, w_ref, gpost_ref, gffn_ref, x1_ref, hn_ref):
    mix = (jnp.dot(a_ref[...], w_ref[:GMLP_W, :], preferred_element_type=F32)
           + jnp.dot(b_ref[...], w_ref[GMLP_W:, :], preferred_element_type=F32))
    x1 = x_ref[...] + _rms(mix, gpost_ref[...])
    x1_ref[...] = x1
    hn_ref[...] = _rms(x1, gffn_ref[...]).astype(BF16)


def _outproj(x, out_a, out_b, w_out, g_post, g_ffn, *, tm):
    rows, d = x.shape
    row_blk = lambda w: pl.BlockSpec((tm, w), lambda i: (i, 0))
    const = lambda shape: pl.BlockSpec(shape, lambda i: (0,) * len(shape))
    est = 2 * tm * d * (4 + 4 + 2) + 2 * 2 * tm * GMLP_W * 2 + 2 * w_out.size * 2 + 4 * tm * d * 4
    return pl.pallas_call(
        _outproj_kernel,
        grid=(rows // tm,),
        in_specs=[row_blk(d), row_blk(GMLP_W), row_blk(ATT_W), const(w_out.shape),
                  const((1, d)), const((1, d))],
        out_specs=[row_blk(d), row_blk(d)],
        out_shape=[jax.ShapeDtypeStruct((rows, d), F32), jax.ShapeDtypeStruct((rows, d), BF16)],
        compiler_params=pltpu.CompilerParams(
            dimension_semantics=("arbitrary",), vmem_limit_bytes=_vmem_limit(est)),
        name="outproj",
    )(x, out_a, out_b, w_out, g_post, g_ffn)


def _ffn_kernel(hn_ref, x1_ref, wg_ref, wv_ref, cwg_ref, cwv_ref, cbg_ref, cbv_ref, hg_ref, hv_ref,
                wd_ref, gpost_ref, y_ref, cg_ref, cv_ref, acc_s, carg_s, carv_s, stage_s, order_s,
                *, n_streams, stream_len):
    i = pl.program_id(0)
    j = pl.program_id(1)
    tm = hn_ref.shape[0]
    ft = wg_ref.shape[1]
    hn = hn_ref[...]
    sub = min(FF_SUBTILE, ft)
    slabs = sub // LANES
    per_phase = stream_len // CONV_PHASES

    @pl.when(j == 0)
    def _():
        acc_s[...] = jnp.zeros(acc_s.shape, F32)

    if n_streams == 1:
        @pl.when(i == 0)
        def _():
            pad = jnp.zeros((SUBLANES - (CONV_W - 1), ft), F32)
            carg_s[j] = jnp.concatenate([pad, hg_ref[0]], axis=0)
            carv_s[j] = jnp.concatenate([pad, hv_ref[0]], axis=0)

    def conv_branch(buf, cs, w_ref, cw_ref, cb_ref, h_ref, car_s, cout_ref):
        up = jnp.dot(hn, w_ref[:, cs], preferred_element_type=F32)
        cw = cw_ref[:, cs]
        cb = cb_ref[:, cs]
        hist = car_s[j][:, cs] if n_streams == 1 else None
        halves = []
        for half in range(slabs):
            slab = buf * slabs + half
            lanes = slice(half * LANES, (half + 1) * LANES)
            out_lanes = slice(cs.start + half * LANES, cs.start + (half + 1) * LANES)
            for s in range(n_streams):
                if n_streams == 1:
                    stage_s[slab, s, 0:SUBLANES] = hist[:, lanes]
                else:
                    stage_s[slab, s, SUBLANES - (CONV_W - 1):SUBLANES] = h_ref[s, :, out_lanes]
                stage_s[slab, s, SUBLANES:SUBLANES + stream_len] = (
                    up[s * stream_len:(s + 1) * stream_len, lanes])
            phases = []
            for ph in range(CONV_PHASES):
                y = cb[:, lanes]
                for k in range(CONV_W):
                    first = SUBLANES - (CONV_W - 1) + k + ph
                    rows = [stage_s[slab, s, pl.ds(first, per_phase, stride=CONV_PHASES), :]
                            for s in range(n_streams)]
                    tap = rows[0] if n_streams == 1 else jnp.concatenate(rows, axis=0)
                    y = y + tap * cw[k:k + 1, lanes]
                phases.append(y)
            halves.append(jnp.concatenate(phases, axis=0))
            for s in range(n_streams):
                cout_ref[0, s, :, out_lanes] = stage_s[slab, s, SUBLANES + stream_len - (CONV_W - 1):
                                                       SUBLANES + stream_len]
            if n_streams == 1:
                car_s[j, :, out_lanes] = stage_s[slab, 0, stream_len:stream_len + SUBLANES]
        return jnp.concatenate(halves, axis=1)

    acts = []
    for c0 in range(0, ft, sub):
        cs = slice(c0, c0 + sub)
        buf = 2 * (c0 // sub)
        gate = conv_branch(buf, cs, wg_ref, cwg_ref, cbg_ref, hg_ref, carg_s, cg_ref)
        val = conv_branch(buf + 1, cs, wv_ref, cwv_ref, cbv_ref, hv_ref, carv_s, cv_ref)
        acts.append((_gelu(gate) * val).astype(BF16))
    acc_s[...] += jnp.dot(jnp.concatenate(acts, axis=1), wd_ref[...], preferred_element_type=F32)

    @pl.when(j == pl.num_programs(1) - 1)
    def _():
        fn = _rms(acc_s[...], gpost_ref[...])
        rows_per_phase = tm // CONV_PHASES
        for c in range(fn.shape[1] // LANES):
            for ph in range(CONV_PHASES):
                order_s[c, pl.ds(ph, rows_per_phase, stride=CONV_PHASES), :] = (
                    fn[ph * rows_per_phase:(ph + 1) * rows_per_phase, c * LANES:(c + 1) * LANES])
        f_nat = jnp.concatenate([order_s[c] for c in range(fn.shape[1] // LANES)], axis=1)
        y_ref[...] = x1_ref[...] + f_nat


def _ffn(hn, x1, w_up, conv_w, conv_b, hist, w_down, g_post, *, tm, ft, n_streams):
    rows, d = x1.shape
    d_ff = w_down.shape[0]
    nj = d_ff // ft
    n_tiles = rows // tm
    stream_len = tm // n_streams
    sub = min(FF_SUBTILE, ft)
    assert rows % tm == 0 and d_ff % ft == 0 and (n_streams == 1 or n_tiles == 1)
    assert hist.shape == (n_streams, CONV_W - 1, 2 * d_ff)
    assert stream_len % (SUBLANES * CONV_PHASES) == 0
    gate_col = lambda i, j: (0, j)
    val_col = lambda i, j: (0, nj + j)
    row_blk = lambda w: pl.BlockSpec((tm, w), lambda i, j: (i, 0))
    once_per_tile = lambda w: pl.BlockSpec((tm, w), lambda i, j: (i, 0), pipeline_mode=pl.Buffered(1))
    hist_blk = lambda off: pl.BlockSpec((n_streams, CONV_W - 1, ft), lambda i, j: (0, 0, off + j))
    tail_blk = pl.BlockSpec((1, n_streams, CONV_W - 1, ft), lambda i, j: (i, 0, 0, j))
    tail_shape = jax.ShapeDtypeStruct((n_tiles, n_streams, CONV_W - 1, d_ff), F32)
    est = (tm * d * (2 + 4 + 2 * 4) + tm * d * 4 + 2 * 3 * d * ft * 2 + 16 * tm * FF_SUBTILE * 4
           + 2 * tm * d * 4)
    y, cg, cv = pl.pallas_call(
        functools.partial(_ffn_kernel, n_streams=n_streams, stream_len=stream_len),
        grid=(n_tiles, nj),
        in_specs=[
            once_per_tile(d), once_per_tile(d),
            pl.BlockSpec((d, ft), gate_col), pl.BlockSpec((d, ft), val_col),
            pl.BlockSpec((CONV_W, ft), gate_col), pl.BlockSpec((CONV_W, ft), val_col),
            pl.BlockSpec((1, ft), gate_col), pl.BlockSpec((1, ft), val_col),
            hist_blk(0), hist_blk(nj),
            pl.BlockSpec((ft, d), lambda i, j: (j, 0)),
            pl.BlockSpec((1, d), lambda i, j: (0, 0)),
        ],
        out_specs=[row_blk(d), tail_blk, tail_blk],
        out_shape=[jax.ShapeDtypeStruct((rows, d), F32), tail_shape, tail_shape],
        scratch_shapes=[pltpu.VMEM((tm, d), F32),
                        pltpu.VMEM((nj, SUBLANES, ft), F32), pltpu.VMEM((nj, SUBLANES, ft), F32),
                        pltpu.VMEM((2 * (ft // sub) * (sub // LANES), n_streams,
                                    SUBLANES + stream_len, LANES), F32),
                        pltpu.VMEM((d // LANES, tm, LANES), F32)],
        compiler_params=pltpu.CompilerParams(
            dimension_semantics=("arbitrary", "arbitrary"), vmem_limit_bytes=_vmem_limit(est)),
        name="convffn",
    )(hn, x1, w_up, w_up, conv_w, conv_w, conv_b, conv_b, hist, hist, w_down, g_post)
    return y, jnp.concatenate([cg[-1], cv[-1]], axis=-1)


def _layer(x, pos, hist, attend, lw, *, gmlp_len, transposed, emit_vn, n_streams, tm):
    (g_mix_pre, w_in, gv, w_s, b_s, subln_g, w_out, g_mix_post, g_ffn_pre, w_up, conv_w, conv_b,
     w_down, g_ffn_post) = lw
    res = _inproj(x, pos, g_mix_pre, w_in, gv, w_s, b_s, gmlp_len=gmlp_len, transposed=transposed,
                  emit_vn=emit_vn, tm=min(PROJ_ROW_TILE, tm))
    out_a, q, k32, kbf, v32, vbf = res[:6]
    vn = res[6] if emit_vn else None
    out_b = attend(q, kbf, vbf)
    x1, hn2 = _outproj(x, out_a, out_b, w_out, g_mix_post, g_ffn_pre, tm=tm)
    y, new_hist = _ffn(hn2, x1, w_up, conv_w, conv_b, hist, w_down, g_ffn_post,
                       tm=tm, ft=FF_TILE, n_streams=n_streams)
    return y, k32, v32, vn, new_hist


def kernel(x_prompt, x_sample, cache_k, cache_v, state_conv, g_mix_pre, w_in, gmlp_v_gain, gmlp_w_s,
           gmlp_b_s, lambda_q1, lambda_k1, lambda_q2, lambda_k2, subln_g, w_out, g_mix_post,
           g_ffn_pre, w_up, conv_w, conv_b, w_down, g_ffn_post):
    b_p, s_p, d = x_prompt.shape
    b_s, t_new, _ = x_sample.shape
    depth = w_in.shape[0]
    past_len = cache_k.shape[2]
    d_ff = w_down.shape[1]
    assert b_p == 1
    pos_p = (0, s_p, 1)
    pos_s = (past_len, t_new, b_s)
    zero_hist = jnp.zeros((b_p, CONV_W - 1, 2 * d_ff), F32)
    row = lambda a: a.reshape(1, -1)

    yp = x_prompt.reshape(b_p * s_p, d)
    ys = x_sample.reshape(b_s * t_new, d)
    outs = [[] for _ in range(7)]
    for l in range(depth):
        lam_init = 0.8 - 0.6 * math.exp(-0.3 * l)
        lam_vecs = tuple(row(a[l]) for a in (lambda_q1, lambda_k1, lambda_q2, lambda_k2))
        sub_g = row(subln_g[l])
        lw = (row(g_mix_pre[l]), w_in[l].astype(BF16), row(gmlp_v_gain[l]), gmlp_w_s[l], gmlp_b_s[l],
              sub_g, w_out[l].astype(BF16), row(g_mix_post[l]), row(g_ffn_pre[l]),
              w_up[l].astype(BF16), conv_w[l], row(conv_b[l]), w_down[l].astype(BF16),
              row(g_ffn_post[l]))

        tq = min(ATT_Q_TILE, s_p)
        attend_p = lambda q, k, v: _prompt_attention(
            q, k, v, lam_vecs, sub_g, lam_init=lam_init, tq=tq, tk=min(ATT_K_TILE, tq))
        yp, kp, vp, _, cp = _layer(yp, pos_p, zero_hist, attend_p, lw, gmlp_len=GROUP_DIM,
                                   transposed=True, emit_vn=False, n_streams=1,
                                   tm=min(ROW_TILE, s_p))

        ck = cache_k[l].reshape(b_s, past_len, ATT_W)
        cv = cache_v[l]
        attend_s = lambda q, k, v: _sample_attention(
            q, k, v, ck, cv, lam_vecs, sub_g, lam_init=lam_init, t_new=t_new)
        ys, ks, vs, gs, cs = _layer(ys, pos_s, state_conv[l], attend_s, lw, gmlp_len=t_new,
                                    transposed=False, emit_vn=True, n_streams=b_s,
                                    tm=b_s * t_new)

        for acc, val in zip(outs, (
                kp.reshape(b_p, s_p, N_HEADS, 2, HEAD_DIM), vp.reshape(b_p, s_p, N_HEADS, HEAD_W), cp,
                ks.reshape(b_s, t_new, N_HEADS, 2, HEAD_DIM), vs.reshape(b_s, t_new, N_HEADS, HEAD_W),
                gs.reshape(b_s, t_new, N_GROUPS, GROUP_DIM), cs)):
            acc.append(val)

    return (yp.reshape(b_p, s_p, d), ys.reshape(b_s, t_new, d)) + tuple(jnp.stack(o) for o in outs)
```

```python
import functools
import math

import jax
import jax.numpy as jnp
from jax import lax
from jax.experimental import pallas as pl
from jax.experimental.pallas import tpu as pltpu

F32 = jnp.float32
BF16 = jnp.bfloat16

N_HEADS = 8
HEAD_DIM = 64
HEAD_W = 2 * HEAD_DIM
CHUNK = 64
ROT_DIM = HEAD_DIM // 4
ROT_HALF = ROT_DIM // 2
ROPE_THETA = 500000.0
ATT_SCALE = HEAD_DIM ** -0.5
N_GROUPS = 8
GROUP_DIM = 128
GMLP_W = N_GROUPS * GROUP_DIM
ATT_W = N_HEADS * HEAD_W
CONV_W = 3
EPS = 1e-6
LOG2E = 1.4426950408889634

LANES = 128
SUBLANES = 8
VMEM_BYTES_V7X = 64 * 1024 * 1024

ROW_TILE = 512
PROJ_ROW_TILE = 512
FF_TILE = 512
FF_SUBTILE = 256
CONV_PHASES = 4
ATT_Q_TILE = 512
ATT_K_TILE = 512


def _vmem_limit(estimate_bytes):
    return int(min(VMEM_BYTES_V7X - (4 << 20), max(32 << 20, estimate_bytes)))


def _gelu(x):
    a = -2.0 * math.sqrt(2.0 / math.pi) * LOG2E
    return x / (1.0 + jnp.exp2(x * (a + (a * 0.044715) * (x * x))))


def _rms(x, g):
    return x * lax.rsqrt(jnp.mean(x * x, axis=-1, keepdims=True) + EPS) * g


def _rope(z, cos, sin_lo, sin_hi):
    outs = []
    for s in range(z.shape[1] // LANES):
        xs = z[:, s * LANES:(s + 1) * LANES]
        nxt = pltpu.roll(xs, LANES - ROT_HALF, axis=1)
        prv = pltpu.roll(xs, ROT_HALF, axis=1)
        outs.append(xs * cos + nxt * sin_lo + prv * sin_hi)
    return jnp.concatenate(outs, axis=1)


def _inproj_kernel(x_ref, gpre_ref, w_ref, gv_ref, ws_ref, bs_ref, cos_ref, slo_ref, shi_ref,
                   *refs, gmlp_len, transposed, emit_vn, k_tile):
    if emit_vn:
        outa_ref, q_ref, k32_ref, kbf_ref, v32_ref, vbf_ref, vn_ref = refs
    else:
        outa_ref, q_ref, k32_ref, kbf_ref, v32_ref, vbf_ref = refs
        vn_ref = None
    tm = x_ref.shape[0]
    hn = _rms(x_ref[...], gpre_ref[...]).astype(BF16)
    block = lambda n: jnp.dot(hn, w_ref[:, n * GMLP_W:(n + 1) * GMLP_W],
                              preferred_element_type=F32)

    u = _gelu(block(0))
    vn = _rms(_gelu(block(1)), gv_ref[...])
    if emit_vn:
        vn_ref[...] = vn
    vnb = vn.astype(BF16)
    ln = gmlp_len
    causal = (lax.broadcasted_iota(jnp.int32, (ln, ln), 1)
              <= lax.broadcasted_iota(jnp.int32, (ln, ln), 0))
    for g in range(N_GROUPS):
        wg = jnp.where(causal, ws_ref[g], 0.0).astype(BF16)
        cols = slice(g * GROUP_DIM, (g + 1) * GROUP_DIM)
        bias = bs_ref[:, cols]
        for ch in range(tm // ln):
            rows = slice(ch * ln, (ch + 1) * ln)
            sg = jnp.dot(wg, vnb[rows, cols], preferred_element_type=F32) + bias
            outa_ref[rows, cols] = (u[rows, cols] * sg).astype(BF16)

    cos, slo, shi = cos_ref[...], slo_ref[...], shi_ref[...]
    q = _rope(block(2), cos, slo, shi) * (ATT_SCALE * LOG2E)
    q_ref[...] = q.T.astype(BF16) if transposed else q.astype(BF16)

    k = _rope(block(3), cos, slo, shi)
    k32_ref[...] = k
    kbf_ref[...] = k.astype(BF16)

    v = block(4)
    v32_ref[...] = v
    if not transposed:
        vbf_ref[...] = v.astype(BF16)
    else:
        vt = v.T.astype(BF16)
        if k_tile >= tm:
            vbf_ref[0] = vt
        else:
            for cb in range(tm // k_tile):
                vbf_ref[cb] = vt[:, cb * k_tile:(cb + 1) * k_tile]


def _rope_tables(start, count):
    lane = jnp.arange(LANES)
    in_block = lane % HEAD_DIM
    inv_freq = ROPE_THETA ** (-jnp.arange(ROT_HALF, dtype=F32) / ROT_HALF)
    freq = inv_freq[lane % ROT_HALF][None, :]
    a0 = start // LANES
    n_hi = -(-(start + count) // LANES) - a0
    hi = ((a0 + jnp.arange(n_hi)) * LANES).astype(F32)[:, None] * freq
    lo = jnp.arange(LANES).astype(F32)[:, None] * freq
    cos_hi, sin_hi = jnp.cos(hi)[:, None, :], jnp.sin(hi)[:, None, :]
    cos_lo, sin_lo = jnp.cos(lo)[None], jnp.sin(lo)[None]
    first = start - a0 * LANES
    rows = lambda t: t.reshape(n_hi * LANES, LANES)[first:first + count]
    cos = rows(cos_hi * cos_lo - sin_hi * sin_lo)
    sin = rows(sin_hi * cos_lo + cos_hi * sin_lo)
    rot_lo = in_block < ROT_HALF
    rot_hi = (in_block >= ROT_HALF) & (in_block < ROT_DIM)
    return (jnp.where(rot_lo | rot_hi, cos, 1.0), jnp.where(rot_lo, -sin, 0.0),
            jnp.where(rot_hi, sin, 0.0))


def _inproj(x, pos, g_pre, w_in, gv, w_s, b_s, *, gmlp_len, transposed, emit_vn, tm):
    rows, d = x.shape
    assert rows % tm == 0 and tm % gmlp_len == 0
    k_tile = min(ATT_K_TILE, rows)
    assert k_tile % tm == 0 or tm % k_tile == 0
    start, count, repeats = pos
    assert count * repeats == rows
    cos, slo, shi = (jnp.tile(t, (repeats, 1)) for t in _rope_tables(start, count))
    ws = w_s[:, :gmlp_len, :gmlp_len]
    bs = jnp.repeat(b_s[:, :gmlp_len].T, GROUP_DIM, axis=1)

    row_blk = lambda w: pl.BlockSpec((tm, w), lambda i: (i, 0))
    const = lambda shape: pl.BlockSpec(shape, lambda i: (0,) * len(shape), pipeline_mode=pl.Buffered(1))
    in_specs = [
        row_blk(d), const((1, d)), const(w_in.shape),
        const((1, GMLP_W)), const(ws.shape), const(bs.shape),
        row_blk(LANES), row_blk(LANES), row_blk(LANES),
    ]
    if transposed:
        q_shape, q_spec = (ATT_W, rows), pl.BlockSpec((ATT_W, tm), lambda i: (0, i))
        v_shape = (rows // k_tile, ATT_W, k_tile)
        if k_tile >= tm:
            per = k_tile // tm
            v_spec = pl.BlockSpec((1, ATT_W, tm), lambda i: (i // per, 0, i % per))
        else:
            v_spec = pl.BlockSpec((tm // k_tile, ATT_W, k_tile), lambda i: (i, 0, 0))
    else:
        q_shape, q_spec = (rows, ATT_W), row_blk(ATT_W)
        v_shape, v_spec = (rows, ATT_W), row_blk(ATT_W)
    out_shape = [
        jax.ShapeDtypeStruct((rows, GMLP_W), BF16), jax.ShapeDtypeStruct(q_shape, BF16),
        jax.ShapeDtypeStruct((rows, ATT_W), F32), jax.ShapeDtypeStruct((rows, ATT_W), BF16),
        jax.ShapeDtypeStruct((rows, ATT_W), F32), jax.ShapeDtypeStruct(v_shape, BF16),
    ]
    out_specs = [row_blk(GMLP_W), q_spec, row_blk(ATT_W), row_blk(ATT_W), row_blk(ATT_W), v_spec]
    if emit_vn:
        out_shape.append(jax.ShapeDtypeStruct((rows, GMLP_W), F32))
        out_specs.append(row_blk(GMLP_W))
    est = (2 * tm * d * 4 + w_in.size * 2 + tm * d * 2
           + 2 * tm * GMLP_W * (2 + 2 + 4 + 2 + 4 + 2 + 4) + 12 * tm * GMLP_W * 4)
    return pl.pallas_call(
        functools.partial(_inproj_kernel, gmlp_len=gmlp_len, transposed=transposed,
                          emit_vn=emit_vn, k_tile=k_tile),
        grid=(rows // tm,),
        in_specs=in_specs, out_specs=out_specs, out_shape=out_shape,
        compiler_params=pltpu.CompilerParams(
            dimension_semantics=("arbitrary",), vmem_limit_bytes=_vmem_limit(est)),
        name="inproj",
    )(x, g_pre, w_in, gv, ws, bs, cos, slo, shi)


def _lambda(lq1_ref, lk1_ref, lq2_ref, lk2_ref, lam_init):
    d1 = jnp.sum(lq1_ref[...] * lk1_ref[...], axis=-1, keepdims=True)
    d2 = jnp.sum(lq2_ref[...] * lk2_ref[...], axis=-1, keepdims=True)
    return jnp.exp(d1) - jnp.exp(d2) + lam_init


def _prompt_attn_kernel(qt_ref, k_ref, vt_ref, lq1_ref, lk1_ref, lq2_ref, lk2_ref, g_ref, out_ref,
                        s_buf, smax_buf, p_buf, alpha_buf, m_s, l_s, acc_s, *, tq, tk, lam_init):
    qi = pl.program_id(1)
    qt = qt_ref[...]
    comp = lax.broadcasted_iota(jnp.int32, qt.shape, 0) < HEAD_DIM
    zero = jnp.zeros_like(qt)
    qs = (jnp.where(comp, qt, zero), jnp.where(comp, zero, qt))

    m_s[...] = jnp.full(m_s.shape, -jnp.inf, F32)
    l_s[...] = jnp.zeros(l_s.shape, F32)
    acc_s[...] = jnp.zeros(acc_s.shape, F32)

    steps_per_tile = tq // tk
    n_full = qi * steps_per_tile
    n_total = n_full + steps_per_tile

    def block_of(j):
        if isinstance(j, int):
            return n_full + j if j < steps_per_tile else j - steps_per_tile
        return jnp.where(j < steps_per_tile, n_full + j, j - steps_per_tile)

    def scores(j):
        kb = block_of(j)
        kblk = k_ref[pl.ds(pl.multiple_of(kb * tk, tk), tk), :]
        masked = isinstance(j, int) and j < steps_per_tile
        if masked:
            key_chunk = (lax.broadcasted_iota(jnp.int32, (tk, tq), 0) + j * tk) // CHUNK
            allowed = key_chunk <= lax.broadcasted_iota(jnp.int32, (tk, tq), 1) // CHUNK
        for c in range(2):
            s = jnp.dot(kblk, qs[c], preferred_element_type=F32)
            if masked:
                s = jnp.where(allowed, s, -jnp.inf)
            s_buf[c] = s
            smax_buf[c] = jnp.max(s, axis=0, keepdims=True)

    def softmax():
        for c in range(2):
            m_old = m_s[c]
            m_new = jnp.maximum(m_old, smax_buf[c])
            alpha = jnp.exp2(m_old - m_new)
            alpha_buf[c] = alpha
            p = jnp.exp2(s_buf[c] - m_new)
            l_s[c] = alpha * l_s[c] + jnp.sum(p, axis=0, keepdims=True)
            p_buf[c] = p.astype(BF16)
            m_s[c] = m_new

    def values(j):
        vtb = vt_ref[block_of(j)]
        for c in range(2):
            acc_s[c] = alpha_buf[c] * acc_s[c] + jnp.dot(vtb, p_buf[c], preferred_element_type=F32)

    def step(u, with_values=True, with_softmax=True):
        if with_values:
            values(u - 2)
        if with_softmax:
            softmax()
        scores(u)

    def run_steps(first, stop):
        count = stop - first

        def pair(v, carry):
            step(first + 2 * v)
            step(first + 2 * v + 1)
            return carry

        lax.fori_loop(0, count // 2, pair, 0)
        pl.when(count % 2 == 1)(lambda: step(stop - 1))

    def drain():
        values(n_total - 2)
        softmax()

    if steps_per_tile >= 2:
        for u in range(steps_per_tile):
            step(u, with_values=u >= 2, with_softmax=u >= 1)
        run_steps(steps_per_tile, n_total)
        drain()
        values(n_total - 1)
    else:
        @pl.when(qi > 0)
        def _():
            step(0, with_values=False, with_softmax=False)
            step(1, with_values=False)
            run_steps(2, n_total)
            drain()
            values(n_total - 1)

        @pl.when(qi == 0)
        def _():
            step(0, with_values=False, with_softmax=False)
            softmax()
            values(0)

    lam = _lambda(lq1_ref, lk1_ref, lq2_ref, lk2_ref, lam_init)
    o = acc_s[0] / l_s[0] - lam * (acc_s[1] / l_s[1])
    on = o * lax.rsqrt(jnp.mean(o * o, axis=0, keepdims=True) + EPS)
    out_ref[...] = (on.T * (g_ref[...] * (1.0 - lam_init))).astype(BF16)


def _prompt_attention(qt, kbf, vt, lam_vecs, subln_g, *, lam_init, tq, tk):
    rows = kbf.shape[0]
    assert rows % tq == 0 and tq % tk == 0 and tk % CHUNK == 0 and vt.shape[2] == tk
    vec = pl.BlockSpec((1, HEAD_DIM), lambda h, i: (0, 0))
    est = 2 * 2 * rows * HEAD_W * 2 + 2 * HEAD_W * tq * 4 + 12 * tk * tq * 4
    return pl.pallas_call(
        functools.partial(_prompt_attn_kernel, tq=tq, tk=tk, lam_init=lam_init),
        grid=(N_HEADS, rows // tq),
        in_specs=[
            pl.BlockSpec((HEAD_W, tq), lambda h, i: (h, i)),
            pl.BlockSpec((rows, HEAD_W), lambda h, i: (0, h)),
            pl.BlockSpec((rows // tk, HEAD_W, tk), lambda h, i: (0, h, 0)),
            vec, vec, vec, vec,
            pl.BlockSpec((1, HEAD_W), lambda h, i: (0, 0)),
        ],
        out_specs=pl.BlockSpec((tq, HEAD_W), lambda h, i: (i, h)),
        out_shape=jax.ShapeDtypeStruct((rows, ATT_W), BF16),
        scratch_shapes=[pltpu.VMEM((2, tk, tq), F32), pltpu.VMEM((2, 1, tq), F32),
                        pltpu.VMEM((2, tk, tq), BF16), pltpu.VMEM((2, 1, tq), F32),
                        pltpu.VMEM((2, 1, tq), F32), pltpu.VMEM((2, 1, tq), F32),
                        pltpu.VMEM((2, HEAD_W, tq), F32)],
        compiler_params=pltpu.CompilerParams(
            dimension_semantics=("arbitrary", "arbitrary"), vmem_limit_bytes=_vmem_limit(est)),
        name="prompt_attn",
    )(qt, kbf, vt, *lam_vecs, subln_g)


def _sample_attn_kernel(q_ref, kc_ref, vc_ref, kn_ref, vn_ref, lq1_ref, lk1_ref, lq2_ref, lk2_ref,
                        g_ref, out_ref, *, past_len, lam_init):
    t = q_ref.shape[0]
    comp = lax.broadcasted_iota(jnp.int32, (t, HEAD_W), 1) < HEAD_DIM
    q_chunk = (past_len + lax.broadcasted_iota(jnp.int32, (t, 1), 0)) // CHUNK
    ok_c = lax.broadcasted_iota(jnp.int32, (t, past_len), 1) // CHUNK <= q_chunk
    ok_n = (past_len + lax.broadcasted_iota(jnp.int32, (t, t), 1)) // CHUNK <= q_chunk
    nt = (((1,), (1,)), ((), ()))
    lam = _lambda(lq1_ref, lk1_ref, lq2_ref, lk2_ref, lam_init)
    for h in range(N_HEADS):
        cols = slice(h * HEAD_W, (h + 1) * HEAD_W)
        q = q_ref[:, cols]
        zero = jnp.zeros_like(q)
        qs = (jnp.where(comp, q, zero), jnp.where(comp, zero, q))
        kc = kc_ref[0, :, cols].astype(BF16)
        vc = vc_ref[0, :, h, :].astype(BF16)
        kn = kn_ref[:, cols]
        vn = vn_ref[:, cols]
        outs = []
        for c in range(2):
            sc = lax.dot_general(qs[c], kc, nt, preferred_element_type=F32)
            sn = lax.dot_general(qs[c], kn, nt, preferred_element_type=F32)
            sc = jnp.where(ok_c, sc, -jnp.inf)
            sn = jnp.where(ok_n, sn, -jnp.inf)
            m = jnp.maximum(jnp.max(sc, axis=-1, keepdims=True), jnp.max(sn, axis=-1, keepdims=True))
            pc = jnp.exp2(sc - m)
            pn = jnp.exp2(sn - m)
            den = jnp.sum(pc, axis=-1, keepdims=True) + jnp.sum(pn, axis=-1, keepdims=True)
            num = (jnp.dot(pc.astype(BF16), vc, preferred_element_type=F32)
                   + jnp.dot(pn.astype(BF16), vn, preferred_element_type=F32))
            outs.append(num / den)
        o = outs[0] - lam * outs[1]
        out_ref[:, cols] = (_rms(o, g_ref[...]) * (1.0 - lam_init)).astype(BF16)


def _sample_attention(q, kbf, vbf, cache_k, cache_v, lam_vecs, subln_g, *, lam_init, t_new):
    n_streams, past_len = cache_k.shape[:2]
    vec = pl.BlockSpec((1, HEAD_DIM), lambda b: (0, 0))
    new_blk = pl.BlockSpec((t_new, ATT_W), lambda b: (b, 0))
    est = 2 * 2 * past_len * ATT_W * 4 + 16 * t_new * past_len * 4
    return pl.pallas_call(
        functools.partial(_sample_attn_kernel, past_len=past_len, lam_init=lam_init),
        grid=(n_streams,),
        in_specs=[new_blk, pl.BlockSpec((1, past_len, ATT_W), lambda b: (b, 0, 0)),
                  pl.BlockSpec((1, past_len, N_HEADS, HEAD_W), lambda b: (b, 0, 0, 0)),
                  new_blk, new_blk, vec, vec, vec, vec,
                  pl.BlockSpec((1, HEAD_W), lambda b: (0, 0))],
        out_specs=new_blk,
        out_shape=jax.ShapeDtypeStruct((n_streams * t_new, ATT_W), BF16),
        compiler_params=pltpu.CompilerParams(
            dimension_semantics=("arbitrary",), vmem_limit_bytes=_vmem_limit(est)),
        name="sample_attn",
    )(q, cache_k, cache_v, kbf, vbf, *lam_vecs, subln_g)


def _outproj_kernel(x_ref, a_ref, b_ref, w_ref, gpost_ref, gffn_ref, x1_ref, hn_ref):
    mix = (jnp.dot(a_ref[...], w_ref[:GMLP_W, :], preferred_element_type=F32)
           + jnp.dot(b_ref[...], w_ref[GMLP_W:, :], preferred_element_type=F32))
    x1 = x_ref[...] + _rms(mix, gpost_ref[...])
    x1_ref[...] = x1
    hn_ref[...] = _rms(x1, gffn_ref[...]).astype(BF16)


def _outproj(x, out_a, out_b, w_out, g_post, g_ffn, *, tm):
    rows, d = x.shape
    row_blk = lambda w: pl.BlockSpec((tm, w), lambda i: (i, 0))
    const = lambda shape: pl.BlockSpec(shape, lambda i: (0,) * len(shape))
    est = 2 * tm * d * (4 + 4 + 2) + 2 * 2 * tm * GMLP_W * 2 + 2 * w_out.size * 2 + 4 * tm * d * 4
    return pl.pallas_call(
        _outproj_kernel,
        grid=(rows // tm,),
        in_specs=[row_blk(d), row_blk(GMLP_W), row_blk(ATT_W), const(w_out.shape),
                  const((1, d)), const((1, d))],
        out_specs=[row_blk(d), row_blk(d)],
        out_shape=[jax.ShapeDtypeStruct((rows, d), F32), jax.ShapeDtypeStruct((rows, d), BF16)],
        compiler_params=pltpu.CompilerParams(
            dimension_semantics=("arbitrary",), vmem_limit_bytes=_vmem_limit(est)),
        name="outproj",
    )(x, out_a, out_b, w_out, g_post, g_ffn)


def _ffn_kernel(hn_ref, x1_ref, wg_ref, wv_ref, cwg_ref, cwv_ref, cbg_ref, cbv_ref, hg_ref, hv_ref,
                wd_ref, gpost_ref, y_ref, cg_ref, cv_ref, acc_s, carg_s, carv_s, stage_s, order_s,
                *, n_streams, stream_len):
    i = pl.program_id(0)
    j = pl.program_id(1)
    tm = hn_ref.shape[0]
    ft = wg_ref.shape[1]
    hn = hn_ref[...]
    sub = min(FF_SUBTILE, ft)
    slabs = sub // LANES
    per_phase = stream_len // CONV_PHASES

    @pl.when(j == 0)
    def _():
        acc_s[...] = jnp.zeros(acc_s.shape, F32)

    if n_streams == 1:
        @pl.when(i == 0)
        def _():
            pad = jnp.zeros((SUBLANES - (CONV_W - 1), ft), F32)
            carg_s[j] = jnp.concatenate([pad, hg_ref[0]], axis=0)
            carv_s[j] = jnp.concatenate([pad, hv_ref[0]], axis=0)

    def conv_branch(buf, cs, w_ref, cw_ref, cb_ref, h_ref, car_s, cout_ref):
        up = jnp.dot(hn, w_ref[:, cs], preferred_element_type=F32)
        cw = cw_ref[:, cs]
        cb = cb_ref[:, cs]
        hist = car_s[j][:, cs] if n_streams == 1 else None
        halves = []
        for half in range(slabs):
            slab = buf * slabs + half
            lanes = slice(half * LANES, (half + 1) * LANES)
            out_lanes = slice(cs.start + half * LANES, cs.start + (half + 1) * LANES)
            for s in range(n_streams):
                if n_streams == 1:
                    stage_s[slab, s, 0:SUBLANES] = hist[:, lanes]
                else:
                    stage_s[slab, s, SUBLANES - (CONV_W - 1):SUBLANES] = h_ref[s, :, out_lanes]
                stage_s[slab, s, SUBLANES:SUBLANES + stream_len] = (
                    up[s * stream_len:(s + 1) * stream_len, lanes])
            phases = []
            for ph in range(CONV_PHASES):
                y = cb[:, lanes]
                for k in range(CONV_W):
                    first = SUBLANES - (CONV_W - 1) + k + ph
                    rows = [stage_s[slab, s, pl.ds(first, per_phase, stride=CONV_PHASES), :]
                            for s in range(n_streams)]
                    tap = rows[0] if n_streams == 1 else jnp.concatenate(rows, axis=0)
                    y = y + tap * cw[k:k + 1, lanes]
                phases.append(y)
            halves.append(jnp.concatenate(phases, axis=0))
            for s in range(n_streams):
                cout_ref[0, s, :, out_lanes] = stage_s[slab, s, SUBLANES + stream_len - (CONV_W - 1):
                                                       SUBLANES + stream_len]
            if n_streams == 1:
                car_s[j, :, out_lanes] = stage_s[slab, 0, stream_len:stream_len + SUBLANES]
        return jnp.concatenate(halves, axis=1)

    acts = []
    for c0 in range(0, ft, sub):
        cs = slice(c0, c0 + sub)
        buf = 2 * (c0 // sub)
        gate = conv_branch(buf, cs, wg_ref, cwg_ref, cbg_ref, hg_ref, carg_s, cg_ref)
        val = conv_branch(buf + 1, cs, wv_ref, cwv_ref, cbv_ref, hv_ref, carv_s, cv_ref)
        acts.append((_gelu(gate) * val).astype(BF16))
    acc_s[...] += jnp.dot(jnp.concatenate(acts, axis=1), wd_ref[...], preferred_element_type=F32)

    @pl.when(j == pl.num_programs(1) - 1)
    def _():
        fn = _rms(acc_s[...], gpost_ref[...])
        rows_per_phase = tm // CONV_PHASES
        for c in range(fn.shape[1] // LANES):
            for ph in range(CONV_PHASES):
                order_s[c, pl.ds(ph, rows_per_phase, stride=CONV_PHASES), :] = (
                    fn[ph * rows_per_phase:(ph + 1) * rows_per_phase, c * LANES:(c + 1) * LANES])
        f_nat = jnp.concatenate([order_s[c] for c in range(fn.shape[1] // LANES)], axis=1)
        y_ref[...] = x1_ref[...] + f_nat


def _ffn(hn, x1, w_up, conv_w, conv_b, hist, w_down, g_post, *, tm, ft, n_streams):
    rows, d = x1.shape
    d_ff = w_down.shape[0]
    nj = d_ff // ft
    n_tiles = rows // tm
    stream_len = tm // n_streams
    sub = min(FF_SUBTILE, ft)
    assert rows % tm == 0 and d_ff % ft == 0 and (n_streams == 1 or n_tiles == 1)
    assert hist.shape == (n_streams, CONV_W - 1, 2 * d_ff)
    assert stream_len % (SUBLANES * CONV_PHASES) == 0
    gate_col = lambda i, j: (0, j)
    val_col = lambda i, j: (0, nj + j)
    row_blk = lambda w: pl.BlockSpec((tm, w), lambda i, j: (i, 0))
    hist_blk = lambda off: pl.BlockSpec((n_streams, CONV_W - 1, ft), lambda i, j: (0, 0, off + j))
    tail_blk = pl.BlockSpec((1, n_streams, CONV_W - 1, ft), lambda i, j: (i, 0, 0, j))
    tail_shape = jax.ShapeDtypeStruct((n_tiles, n_streams, CONV_W - 1, d_ff), F32)
    est = (2 * tm * d * (2 + 4 + 4) + tm * d * 4 + 2 * 3 * d * ft * 2 + 16 * tm * FF_SUBTILE * 4
           + 2 * tm * d * 4)
    y, cg, cv = pl.pallas_call(
        functools.partial(_ffn_kernel, n_streams=n_streams, stream_len=stream_len),
        grid=(n_tiles, nj),
        in_specs=[
            row_blk(d), row_blk(d),
            pl.BlockSpec((d, ft), gate_col), pl.BlockSpec((d, ft), val_col),
            pl.BlockSpec((CONV_W, ft), gate_col), pl.BlockSpec((CONV_W, ft), val_col),
            pl.BlockSpec((1, ft), gate_col), pl.BlockSpec((1, ft), val_col),
            hist_blk(0), hist_blk(nj),
            pl.BlockSpec((ft, d), lambda i, j: (j, 0)),
            pl.BlockSpec((1, d), lambda i, j: (0, 0)),
        ],
        out_specs=[row_blk(d), tail_blk, tail_blk],
        out_shape=[jax.ShapeDtypeStruct((rows, d), F32), tail_shape, tail_shape],
        scratch_shapes=[pltpu.VMEM((tm, d), F32),
                        pltpu.VMEM((nj, SUBLANES, ft), F32), pltpu.VMEM((nj, SUBLANES, ft), F32),
                        pltpu.VMEM((2 * (ft // sub) * (sub // LANES), n_streams,
                                    SUBLANES + stream_len, LANES), F32),
                        pltpu.VMEM((d // LANES, tm, LANES), F32)],
        compiler_params=pltpu.CompilerParams(
            dimension_semantics=("arbitrary", "arbitrary"), vmem_limit_bytes=_vmem_limit(est)),
        name="convffn",
    )(hn, x1, w_up, w_up, conv_w, conv_w, conv_b, conv_b, hist, hist, w_down, g_post)
    return y, jnp.concatenate([cg[-1], cv[-1]], axis=-1)


def _layer(x, pos, hist, attend, lw, *, gmlp_len, transposed, emit_vn, n_streams, tm):
    (g_mix_pre, w_in, gv, w_s, b_s, subln_g, w_out, g_mix_post, g_ffn_pre, w_up, conv_w, conv_b,
     w_down, g_ffn_post) = lw
    res = _inproj(x, pos, g_mix_pre, w_in, gv, w_s, b_s, gmlp_len=gmlp_len, transposed=transposed,
                  emit_vn=emit_vn, tm=min(PROJ_ROW_TILE, tm))
    out_a, q, k32, kbf, v32, vbf = res[:6]
    vn = res[6] if emit_vn else None
    out_b = attend(q, kbf, vbf)
    x1, hn2 = _outproj(x, out_a, out_b, w_out, g_mix_post, g_ffn_pre, tm=tm)
    y, new_hist = _ffn(hn2, x1, w_up, conv_w, conv_b, hist, w_down, g_ffn_post,
                       tm=tm, ft=FF_TILE, n_streams=n_streams)
    return y, k32, v32, vn, new_hist


def kernel(x_prompt, x_sample, cache_k, cache_v, state_conv, g_mix_pre, w_in, gmlp_v_gain, gmlp_w_s,
           gmlp_b_s, lambda_q1, lambda_k1, lambda_q2, lambda_k2, subln_g, w_out, g_mix_post,
           g_ffn_pre, w_up, conv_w, conv_b, w_down, g_ffn_post):
    b_p, s_p, d = x_prompt.shape
    b_s, t_new, _ = x_sample.shape
    depth = w_in.shape[0]
    past_len = cache_k.shape[2]
    d_ff = w_down.shape[1]
    assert b_p == 1
    pos_p = (0, s_p, 1)
    pos_s = (past_len, t_new, b_s)
    zero_hist = jnp.zeros((b_p, CONV_W - 1, 2 * d_ff), F32)
    row = lambda a: a.reshape(1, -1)

    yp = x_prompt.reshape(b_p * s_p, d)
    ys = x_sample.reshape(b_s * t_new, d)
    outs = [[] for _ in range(7)]
    for l in range(depth):
        lam_init = 0.8 - 0.6 * math.exp(-0.3 * l)
        lam_vecs = tuple(row(a[l]) for a in (lambda_q1, lambda_k1, lambda_q2, lambda_k2))
        sub_g = row(subln_g[l])
        lw = (row(g_mix_pre[l]), w_in[l].astype(BF16), row(gmlp_v_gain[l]), gmlp_w_s[l], gmlp_b_s[l],
              sub_g, w_out[l].astype(BF16), row(g_mix_post[l]), row(g_ffn_pre[l]),
              w_up[l].astype(BF16), conv_w[l], row(conv_b[l]), w_down[l].astype(BF16),
              row(g_ffn_post[l]))

        tq = min(ATT_Q_TILE, s_p)
        attend_p = lambda q, k, v: _prompt_attention(
            q, k, v, lam_vecs, sub_g, lam_init=lam_init, tq=tq, tk=min(ATT_K_TILE, tq))
        yp, kp, vp, _, cp = _layer(yp, pos_p, zero_hist, attend_p, lw, gmlp_len=GROUP_DIM,
                                   transposed=True, emit_vn=False, n_streams=1,
                                   tm=min(ROW_TILE, s_p))

        ck = cache_k[l].reshape(b_s, past_len, ATT_W)
        cv = cache_v[l]
        attend_s = lambda q, k, v: _sample_attention(
            q, k, v, ck, cv, lam_vecs, sub_g, lam_init=lam_init, t_new=t_new)
        ys, ks, vs, gs, cs = _layer(ys, pos_s, state_conv[l], attend_s, lw, gmlp_len=t_new,
                                    transposed=False, emit_vn=True, n_streams=b_s,
                                    tm=b_s * t_new)

        for acc, val in zip(outs, (
                kp.reshape(b_p, s_p, N_HEADS, 2, HEAD_DIM), vp.reshape(b_p, s_p, N_HEADS, HEAD_W), cp,
                ks.reshape(b_s, t_new, N_HEADS, 2, HEAD_DIM), vs.reshape(b_s, t_new, N_HEADS, HEAD_W),
                gs.reshape(b_s, t_new, N_GROUPS, GROUP_DIM), cs)):
            acc.append(val)

    return (yp.reshape(b_p, s_p, d), ys.reshape(b_s, t_new, d)) + tuple(jnp.stack(o) for o in outs)
```

```python
import functools
import math

import jax
import jax.numpy as jnp
from jax import lax
from jax.experimental import pallas as pl
from jax.experimental.pallas import tpu as pltpu

F32 = jnp.float32
BF16 = jnp.bfloat16

N_HEADS = 8
HEAD_DIM = 64
HEAD_W = 2 * HEAD_DIM
CHUNK = 64
ROT_DIM = HEAD_DIM // 4
ROT_HALF = ROT_DIM // 2
ROPE_THETA = 500000.0
ATT_SCALE = HEAD_DIM ** -0.5
N_GROUPS = 8
GROUP_DIM = 128
GMLP_W = N_GROUPS * GROUP_DIM
ATT_W = N_HEADS * HEAD_W
CONV_W = 3
EPS = 1e-6
LOG2E = 1.4426950408889634

LANES = 128
SUBLANES = 8
VMEM_BYTES_V7X = 64 * 1024 * 1024

ROW_TILE = 512
PROJ_ROW_TILE = 512
FF_TILE = 512
FF_SUBTILE = 256
CONV_PHASES = 4
ATT_Q_TILE = 512
ATT_K_TILE = 512


def _vmem_limit(estimate_bytes):
    return int(min(VMEM_BYTES_V7X - (4 << 20), max(32 << 20, estimate_bytes)))


def _gelu(x):
    a = -2.0 * math.sqrt(2.0 / math.pi) * LOG2E
    return x / (1.0 + jnp.exp2(x * (a + (a * 0.044715) * (x * x))))


def _rms(x, g):
    return x * lax.rsqrt(jnp.mean(x * x, axis=-1, keepdims=True) + EPS) * g


def _rope(z, cos, sin_lo, sin_hi):
    outs = []
    for s in range(z.shape[1] // LANES):
        xs = z[:, s * LANES:(s + 1) * LANES]
        nxt = pltpu.roll(xs, LANES - ROT_HALF, axis=1)
        prv = pltpu.roll(xs, ROT_HALF, axis=1)
        outs.append(xs * cos + nxt * sin_lo + prv * sin_hi)
    return jnp.concatenate(outs, axis=1)


def _inproj_kernel(x_ref, gpre_ref, w_ref, gv_ref, ws_ref, bs_ref, cos_ref, slo_ref, shi_ref,
                   *refs, gmlp_len, transposed, emit_vn, k_tile):
    if emit_vn:
        outa_ref, q_ref, k32_ref, kbf_ref, v32_ref, vbf_ref, vn_ref = refs
    else:
        outa_ref, q_ref, k32_ref, kbf_ref, v32_ref, vbf_ref = refs
        vn_ref = None
    tm = x_ref.shape[0]
    hn = _rms(x_ref[...], gpre_ref[...]).astype(BF16)
    block = lambda n: jnp.dot(hn, w_ref[:, n * GMLP_W:(n + 1) * GMLP_W],
                              preferred_element_type=F32)

    u = _gelu(block(0))
    vn = _rms(_gelu(block(1)), gv_ref[...])
    if emit_vn:
        vn_ref[...] = vn
    vnb = vn.astype(BF16)
    ln = gmlp_len
    causal = (lax.broadcasted_iota(jnp.int32, (ln, ln), 1)
              <= lax.broadcasted_iota(jnp.int32, (ln, ln), 0))
    for g in range(N_GROUPS):
        wg = jnp.where(causal, ws_ref[g], 0.0).astype(BF16)
        cols = slice(g * GROUP_DIM, (g + 1) * GROUP_DIM)
        bias = bs_ref[:, cols]
        for ch in range(tm // ln):
            rows = slice(ch * ln, (ch + 1) * ln)
            sg = jnp.dot(wg, vnb[rows, cols], preferred_element_type=F32) + bias
            outa_ref[rows, cols] = (u[rows, cols] * sg).astype(BF16)

    cos, slo, shi = cos_ref[...], slo_ref[...], shi_ref[...]
    q = _rope(block(2), cos, slo, shi) * (ATT_SCALE * LOG2E)
    q_ref[...] = q.T.astype(BF16) if transposed else q.astype(BF16)

    k = _rope(block(3), cos, slo, shi)
    k32_ref[...] = k
    kbf_ref[...] = k.astype(BF16)

    v = block(4)
    v32_ref[...] = v
    if not transposed:
        vbf_ref[...] = v.astype(BF16)
    else:
        vt = v.T.astype(BF16)
        if k_tile >= tm:
            vbf_ref[0] = vt
        else:
            for cb in range(tm // k_tile):
                vbf_ref[cb] = vt[:, cb * k_tile:(cb + 1) * k_tile]


def _rope_tables(start, count):
    lane = jnp.arange(LANES)
    in_block = lane % HEAD_DIM
    inv_freq = ROPE_THETA ** (-jnp.arange(ROT_HALF, dtype=F32) / ROT_HALF)
    freq = inv_freq[lane % ROT_HALF][None, :]
    a0 = start // LANES
    n_hi = -(-(start + count) // LANES) - a0
    hi = ((a0 + jnp.arange(n_hi)) * LANES).astype(F32)[:, None] * freq
    lo = jnp.arange(LANES).astype(F32)[:, None] * freq
    cos_hi, sin_hi = jnp.cos(hi)[:, None, :], jnp.sin(hi)[:, None, :]
    cos_lo, sin_lo = jnp.cos(lo)[None], jnp.sin(lo)[None]
    first = start - a0 * LANES
    rows = lambda t: t.reshape(n_hi * LANES, LANES)[first:first + count]
    cos = rows(cos_hi * cos_lo - sin_hi * sin_lo)
    sin = rows(sin_hi * cos_lo + cos_hi * sin_lo)
    rot_lo = in_block < ROT_HALF
    rot_hi = (in_block >= ROT_HALF) & (in_block < ROT_DIM)
    return (jnp.where(rot_lo | rot_hi, cos, 1.0), jnp.where(rot_lo, -sin, 0.0),
            jnp.where(rot_hi, sin, 0.0))


def _inproj(x, pos, g_pre, w_in, gv, w_s, b_s, *, gmlp_len, transposed, emit_vn, tm):
    rows, d = x.shape
    assert rows % tm == 0 and tm % gmlp_len == 0
    k_tile = min(ATT_K_TILE, rows)
    assert k_tile % tm == 0 or tm % k_tile == 0
    start, count, repeats = pos
    assert count * repeats == rows
    cos, slo, shi = (jnp.tile(t, (repeats, 1)) for t in _rope_tables(start, count))
    ws = w_s[:, :gmlp_len, :gmlp_len]
    bs = jnp.repeat(b_s[:, :gmlp_len].T, GROUP_DIM, axis=1)

    row_blk = lambda w: pl.BlockSpec((tm, w), lambda i: (i, 0))
    const = lambda shape: pl.BlockSpec(shape, lambda i: (0,) * len(shape), pipeline_mode=pl.Buffered(1))
    in_specs = [
        row_blk(d), const((1, d)), const(w_in.shape),
        const((1, GMLP_W)), const(ws.shape), const(bs.shape),
        row_blk(LANES), row_blk(LANES), row_blk(LANES),
    ]
    if transposed:
        q_shape, q_spec = (ATT_W, rows), pl.BlockSpec((ATT_W, tm), lambda i: (0, i))
        v_shape = (rows // k_tile, ATT_W, k_tile)
        if k_tile >= tm:
            per = k_tile // tm
            v_spec = pl.BlockSpec((1, ATT_W, tm), lambda i: (i // per, 0, i % per))
        else:
            v_spec = pl.BlockSpec((tm // k_tile, ATT_W, k_tile), lambda i: (i, 0, 0))
    else:
        q_shape, q_spec = (rows, ATT_W), row_blk(ATT_W)
        v_shape, v_spec = (rows, ATT_W), row_blk(ATT_W)
    out_shape = [
        jax.ShapeDtypeStruct((rows, GMLP_W), BF16), jax.ShapeDtypeStruct(q_shape, BF16),
        jax.ShapeDtypeStruct((rows, ATT_W), F32), jax.ShapeDtypeStruct((rows, ATT_W), BF16),
        jax.ShapeDtypeStruct((rows, ATT_W), F32), jax.ShapeDtypeStruct(v_shape, BF16),
    ]
    out_specs = [row_blk(GMLP_W), q_spec, row_blk(ATT_W), row_blk(ATT_W), row_blk(ATT_W), v_spec]
    if emit_vn:
        out_shape.append(jax.ShapeDtypeStruct((rows, GMLP_W), F32))
        out_specs.append(row_blk(GMLP_W))
    est = (2 * tm * d * 4 + w_in.size * 2 + tm * d * 2
           + 2 * tm * GMLP_W * (2 + 2 + 4 + 2 + 4 + 2 + 4) + 12 * tm * GMLP_W * 4)
    return pl.pallas_call(
        functools.partial(_inproj_kernel, gmlp_len=gmlp_len, transposed=transposed,
                          emit_vn=emit_vn, k_tile=k_tile),
        grid=(rows // tm,),
        in_specs=in_specs, out_specs=out_specs, out_shape=out_shape,
        compiler_params=pltpu.CompilerParams(
            dimension_semantics=("arbitrary",), vmem_limit_bytes=_vmem_limit(est)),
        name="inproj",
    )(x, g_pre, w_in, gv, ws, bs, cos, slo, shi)


def _lambda(lq1_ref, lk1_ref, lq2_ref, lk2_ref, lam_init):
    d1 = jnp.sum(lq1_ref[...] * lk1_ref[...], axis=-1, keepdims=True)
    d2 = jnp.sum(lq2_ref[...] * lk2_ref[...], axis=-1, keepdims=True)
    return jnp.exp(d1) - jnp.exp(d2) + lam_init


def _prompt_attn_kernel(qt_ref, k_ref, vt_ref, lq1_ref, lk1_ref, lq2_ref, lk2_ref, g_ref, out_ref,
                        s_buf, smax_buf, p_buf, alpha_buf, m_s, l_s, acc_s, *, tq, tk, lam_init):
    qi = pl.program_id(1)
    qt = qt_ref[...]
    comp = lax.broadcasted_iota(jnp.int32, qt.shape, 0) < HEAD_DIM
    zero = jnp.zeros_like(qt)
    qs = (jnp.where(comp, qt, zero), jnp.where(comp, zero, qt))

    m_s[...] = jnp.full(m_s.shape, -jnp.inf, F32)
    l_s[...] = jnp.zeros(l_s.shape, F32)
    acc_s[...] = jnp.zeros(acc_s.shape, F32)

    steps_per_tile = tq // tk
    n_full = qi * steps_per_tile
    n_total = n_full + steps_per_tile

    def block_of(j):
        if isinstance(j, int):
            return n_full + j if j < steps_per_tile else j - steps_per_tile
        return jnp.where(j < steps_per_tile, n_full + j, j - steps_per_tile)

    def scores(j):
        kb = block_of(j)
        kblk = k_ref[pl.ds(pl.multiple_of(kb * tk, tk), tk), :]
        masked = isinstance(j, int) and j < steps_per_tile
        if masked:
            key_chunk = (lax.broadcasted_iota(jnp.int32, (tk, tq), 0) + j * tk) // CHUNK
            allowed = key_chunk <= lax.broadcasted_iota(jnp.int32, (tk, tq), 1) // CHUNK
        for c in range(2):
            s = jnp.dot(kblk, qs[c], preferred_element_type=F32)
            if masked:
                s = jnp.where(allowed, s, -jnp.inf)
            s_buf[c] = s
            smax_buf[c] = jnp.max(s, axis=0, keepdims=True)

    def softmax():
        for c in range(2):
            m_old = m_s[c]
            m_new = jnp.maximum(m_old, smax_buf[c])
            alpha = jnp.exp2(m_old - m_new)
            alpha_buf[c] = alpha
            p = jnp.exp2(s_buf[c] - m_new)
            l_s[c] = alpha * l_s[c] + jnp.sum(p, axis=0, keepdims=True)
            p_buf[c] = p.astype(BF16)
            m_s[c] = m_new

    def values(j):
        vtb = vt_ref[block_of(j)]
        for c in range(2):
            acc_s[c] = alpha_buf[c] * acc_s[c] + jnp.dot(vtb, p_buf[c], preferred_element_type=F32)

    def step(u, with_values=True, with_softmax=True):
        if with_values:
            values(u - 2)
        if with_softmax:
            softmax()
        scores(u)

    def run_steps(first, stop):
        count = stop - first

        def pair(v, carry):
            step(first + 2 * v)
            step(first + 2 * v + 1)
            return carry

        lax.fori_loop(0, count // 2, pair, 0)
        pl.when(count % 2 == 1)(lambda: step(stop - 1))

    def drain():
        values(n_total - 2)
        softmax()

    if steps_per_tile >= 2:
        for u in range(steps_per_tile):
            step(u, with_values=u >= 2, with_softmax=u >= 1)
        run_steps(steps_per_tile, n_total)
        drain()
        values(n_total - 1)
    else:
        @pl.when(qi > 0)
        def _():
            step(0, with_values=False, with_softmax=False)
            step(1, with_values=False)
            run_steps(2, n_total)
            drain()
            values(n_total - 1)

        @pl.when(qi == 0)
        def _():
            step(0, with_values=False, with_softmax=False)
            softmax()
            values(0)

    lam = _lambda(lq1_ref, lk1_ref, lq2_ref, lk2_ref, lam_init)
    o = acc_s[0] / l_s[0] - lam * (acc_s[1] / l_s[1])
    on = o * lax.rsqrt(jnp.mean(o * o, axis=0, keepdims=True) + EPS)
    out_ref[...] = (on.T * (g_ref[...] * (1.0 - lam_init))).astype(BF16)


def _prompt_attention(qt, kbf, vt, lam_vecs, subln_g, *, lam_init, tq, tk):
    rows = kbf.shape[0]
    assert rows % tq == 0 and tq % tk == 0 and tk % CHUNK == 0 and vt.shape[2] == tk
    vec = pl.BlockSpec((1, HEAD_DIM), lambda h, i: (0, 0))
    est = 2 * 2 * rows * HEAD_W * 2 + 2 * HEAD_W * tq * 4 + 12 * tk * tq * 4
    return pl.pallas_call(
        functools.partial(_prompt_attn_kernel, tq=tq, tk=tk, lam_init=lam_init),
        grid=(N_HEADS, rows // tq),
        in_specs=[
            pl.BlockSpec((HEAD_W, tq), lambda h, i: (h, i)),
            pl.BlockSpec((rows, HEAD_W), lambda h, i: (0, h)),
            pl.BlockSpec((rows // tk, HEAD_W, tk), lambda h, i: (0, h, 0)),
            vec, vec, vec, vec,
            pl.BlockSpec((1, HEAD_W), lambda h, i: (0, 0)),
        ],
        out_specs=pl.BlockSpec((tq, HEAD_W), lambda h, i: (i, h)),
        out_shape=jax.ShapeDtypeStruct((rows, ATT_W), BF16),
        scratch_shapes=[pltpu.VMEM((2, tk, tq), F32), pltpu.VMEM((2, 1, tq), F32),
                        pltpu.VMEM((2, tk, tq), BF16), pltpu.VMEM((2, 1, tq), F32),
                        pltpu.VMEM((2, 1, tq), F32), pltpu.VMEM((2, 1, tq), F32),
                        pltpu.VMEM((2, HEAD_W, tq), F32)],
        compiler_params=pltpu.CompilerParams(
            dimension_semantics=("arbitrary", "arbitrary"), vmem_limit_bytes=_vmem_limit(est)),
        name="prompt_attn",
    )(qt, kbf, vt, *lam_vecs, subln_g)


def _sample_attn_kernel(q_ref, kc_ref, vc_ref, kn_ref, vn_ref, lq1_ref, lk1_ref, lq2_ref, lk2_ref,
                        g_ref, out_ref, *, past_len, lam_init):
    t = q_ref.shape[0]
    comp = lax.broadcasted_iota(jnp.int32, (t, HEAD_W), 1) < HEAD_DIM
    q_chunk = (past_len + lax.broadcasted_iota(jnp.int32, (t, 1), 0)) // CHUNK
    ok_c = lax.broadcasted_iota(jnp.int32, (t, past_len), 1) // CHUNK <= q_chunk
    ok_n = (past_len + lax.broadcasted_iota(jnp.int32, (t, t), 1)) // CHUNK <= q_chunk
    nt = (((1,), (1,)), ((), ()))
    lam = _lambda(lq1_ref, lk1_ref, lq2_ref, lk2_ref, lam_init)
    for h in range(N_HEADS):
        cols = slice(h * HEAD_W, (h + 1) * HEAD_W)
        q = q_ref[:, cols]
        zero = jnp.zeros_like(q)
        qs = (jnp.where(comp, q, zero), jnp.where(comp, zero, q))
        kc = kc_ref[0, :, cols].astype(BF16)
        vc = vc_ref[0, :, h, :].astype(BF16)
        kn = kn_ref[:, cols]
        vn = vn_ref[:, cols]
        outs = []
        for c in range(2):
            sc = lax.dot_general(qs[c], kc, nt, preferred_element_type=F32)
            sn = lax.dot_general(qs[c], kn, nt, preferred_element_type=F32)
            sc = jnp.where(ok_c, sc, -jnp.inf)
            sn = jnp.where(ok_n, sn, -jnp.inf)
            m = jnp.maximum(jnp.max(sc, axis=-1, keepdims=True), jnp.max(sn, axis=-1, keepdims=True))
            pc = jnp.exp2(sc - m)
            pn = jnp.exp2(sn - m)
            den = jnp.sum(pc, axis=-1, keepdims=True) + jnp.sum(pn, axis=-1, keepdims=True)
            num = (jnp.dot(pc.astype(BF16), vc, preferred_element_type=F32)
                   + jnp.dot(pn.astype(BF16), vn, preferred_element_type=F32))
            outs.append(num / den)
        o = outs[0] - lam * outs[1]
        out_ref[:, cols] = (_rms(o, g_ref[...]) * (1.0 - lam_init)).astype(BF16)


def _sample_attention(q, kbf, vbf, cache_k, cache_v, lam_vecs, subln_g, *, lam_init, t_new):
    n_streams, past_len = cache_k.shape[:2]
    vec = pl.BlockSpec((1, HEAD_DIM), lambda b: (0, 0))
    new_blk = pl.BlockSpec((t_new, ATT_W), lambda b: (b, 0))
    est = 2 * 2 * past_len * ATT_W * 4 + 16 * t_new * past_len * 4
    return pl.pallas_call(
        functools.partial(_sample_attn_kernel, past_len=past_len, lam_init=lam_init),
        grid=(n_streams,),
        in_specs=[new_blk, pl.BlockSpec((1, past_len, ATT_W), lambda b: (b, 0, 0)),
                  pl.BlockSpec((1, past_len, N_HEADS, HEAD_W), lambda b: (b, 0, 0, 0)),
                  new_blk, new_blk, vec, vec, vec, vec,
                  pl.BlockSpec((1, HEAD_W), lambda b: (0, 0))],
        out_specs=new_blk,
        out_shape=jax.ShapeDtypeStruct((n_streams * t_new, ATT_W), BF16),
        compiler_params=pltpu.CompilerParams(
            dimension_semantics=("arbitrary",), vmem_limit_bytes=_vmem_limit(est)),
        name="sample_attn",
    )(q, cache_k, cache_v, kbf, vbf, *lam_vecs, subln_g)


def _outproj_kernel(x_ref, a_ref, b_ref, w_ref, gpost_ref, gffn_ref, x1_ref, hn_ref):
    half = x_ref.shape[0] // 2
    for rows in (slice(0, half), slice(half, 2 * half)):
        mix = (jnp.dot(a_ref[rows, :], w_ref[:GMLP_W, :], preferred_element_type=F32)
               + jnp.dot(b_ref[rows, :], w_ref[GMLP_W:, :], preferred_element_type=F32))
        x1 = x_ref[rows, :] + _rms(mix, gpost_ref[...])
        x1_ref[rows, :] = x1
        hn_ref[rows, :] = _rms(x1, gffn_ref[...]).astype(BF16)


def _outproj(x, out_a, out_b, w_out, g_post, g_ffn, *, tm):
    rows, d = x.shape
    row_blk = lambda w: pl.BlockSpec((tm, w), lambda i: (i, 0))
    const = lambda shape: pl.BlockSpec(shape, lambda i: (0,) * len(shape))
    est = 2 * tm * d * (4 + 4 + 2) + 2 * 2 * tm * GMLP_W * 2 + 2 * w_out.size * 2 + 4 * tm * d * 4
    return pl.pallas_call(
        _outproj_kernel,
        grid=(rows // tm,),
        in_specs=[row_blk(d), row_blk(GMLP_W), row_blk(ATT_W), const(w_out.shape),
                  const((1, d)), const((1, d))],
        out_specs=[row_blk(d), row_blk(d)],
        out_shape=[jax.ShapeDtypeStruct((rows, d), F32), jax.ShapeDtypeStruct((rows, d), BF16)],
        compiler_params=pltpu.CompilerParams(
            dimension_semantics=("arbitrary",), vmem_limit_bytes=_vmem_limit(est)),
        name="outproj",
    )(x, out_a, out_b, w_out, g_post, g_ffn)


def _ffn_kernel(hn_ref, x1_ref, wg_ref, wv_ref, cwg_ref, cwv_ref, cbg_ref, cbv_ref, hg_ref, hv_ref,
                wd_ref, gpost_ref, y_ref, cg_ref, cv_ref, acc_s, carg_s, carv_s, stage_s, order_s,
                *, n_streams, stream_len):
    i = pl.program_id(0)
    j = pl.program_id(1)
    tm = hn_ref.shape[0]
    ft = wg_ref.shape[1]
    hn = hn_ref[...]
    sub = min(FF_SUBTILE, ft)
    slabs = sub // LANES
    per_phase = stream_len // CONV_PHASES

    @pl.when(j == 0)
    def _():
        acc_s[...] = jnp.zeros(acc_s.shape, F32)

    if n_streams == 1:
        @pl.when(i == 0)
        def _():
            pad = jnp.zeros((SUBLANES - (CONV_W - 1), ft), F32)
            carg_s[j] = jnp.concatenate([pad, hg_ref[0]], axis=0)
            carv_s[j] = jnp.concatenate([pad, hv_ref[0]], axis=0)

    def conv_branch(buf, cs, w_ref, cw_ref, cb_ref, h_ref, car_s, cout_ref):
        up = jnp.dot(hn, w_ref[:, cs], preferred_element_type=F32)
        cw = cw_ref[:, cs]
        cb = cb_ref[:, cs]
        hist = car_s[j][:, cs] if n_streams == 1 else None
        halves = []
        for half in range(slabs):
            slab = buf * slabs + half
            lanes = slice(half * LANES, (half + 1) * LANES)
            out_lanes = slice(cs.start + half * LANES, cs.start + (half + 1) * LANES)
            for s in range(n_streams):
                if n_streams == 1:
                    stage_s[slab, s, 0:SUBLANES] = hist[:, lanes]
                else:
                    stage_s[slab, s, SUBLANES - (CONV_W - 1):SUBLANES] = h_ref[s, :, out_lanes]
                stage_s[slab, s, SUBLANES:SUBLANES + stream_len] = (
                    up[s * stream_len:(s + 1) * stream_len, lanes])
            phases = []
            for ph in range(CONV_PHASES):
                y = cb[:, lanes]
                for k in range(CONV_W):
                    first = SUBLANES - (CONV_W - 1) + k + ph
                    rows = [stage_s[slab, s, pl.ds(first, per_phase, stride=CONV_PHASES), :]
                            for s in range(n_streams)]
                    tap = rows[0] if n_streams == 1 else jnp.concatenate(rows, axis=0)
                    y = y + tap * cw[k:k + 1, lanes]
                phases.append(y)
            halves.append(jnp.concatenate(phases, axis=0))
            for s in range(n_streams):
                cout_ref[0, s, :, out_lanes] = stage_s[slab, s, SUBLANES + stream_len - (CONV_W - 1):
                                                       SUBLANES + stream_len]
            if n_streams == 1:
                car_s[j, :, out_lanes] = stage_s[slab, 0, stream_len:stream_len + SUBLANES]
        return jnp.concatenate(halves, axis=1)

    acts = []
    for c0 in range(0, ft, sub):
        cs = slice(c0, c0 + sub)
        buf = 2 * (c0 // sub)
        gate = conv_branch(buf, cs, wg_ref, cwg_ref, cbg_ref, hg_ref, carg_s, cg_ref)
        val = conv_branch(buf + 1, cs, wv_ref, cwv_ref, cbv_ref, hv_ref, carv_s, cv_ref)
        acts.append((_gelu(gate) * val).astype(BF16))
    acc_s[...] += jnp.dot(jnp.concatenate(acts, axis=1), wd_ref[...], preferred_element_type=F32)

    @pl.when(j == pl.num_programs(1) - 1)
    def _():
        fn = _rms(acc_s[...], gpost_ref[...])
        rows_per_phase = tm // CONV_PHASES
        for c in range(fn.shape[1] // LANES):
            for ph in range(CONV_PHASES):
                order_s[c, pl.ds(ph, rows_per_phase, stride=CONV_PHASES), :] = (
                    fn[ph * rows_per_phase:(ph + 1) * rows_per_phase, c * LANES:(c + 1) * LANES])
        f_nat = jnp.concatenate([order_s[c] for c in range(fn.shape[1] // LANES)], axis=1)
        y_ref[...] = x1_ref[...] + f_nat


def _ffn(hn, x1, w_up, conv_w, conv_b, hist, w_down, g_post, *, tm, ft, n_streams):
    rows, d = x1.shape
    d_ff = w_down.shape[0]
    nj = d_ff // ft
    n_tiles = rows // tm
    stream_len = tm // n_streams
    sub = min(FF_SUBTILE, ft)
    assert rows % tm == 0 and d_ff % ft == 0 and (n_streams == 1 or n_tiles == 1)
    assert hist.shape == (n_streams, CONV_W - 1, 2 * d_ff)
    assert stream_len % (SUBLANES * CONV_PHASES) == 0
    gate_col = lambda i, j: (0, j)
    val_col = lambda i, j: (0, nj + j)
    row_blk = lambda w: pl.BlockSpec((tm, w), lambda i, j: (i, 0))
    hist_blk = lambda off: pl.BlockSpec((n_streams, CONV_W - 1, ft), lambda i, j: (0, 0, off + j))
    tail_blk = pl.BlockSpec((1, n_streams, CONV_W - 1, ft), lambda i, j: (i, 0, 0, j))
    tail_shape = jax.ShapeDtypeStruct((n_tiles, n_streams, CONV_W - 1, d_ff), F32)
    est = (2 * tm * d * (2 + 4 + 4) + tm * d * 4 + 2 * 3 * d * ft * 2 + 16 * tm * FF_SUBTILE * 4
           + 2 * tm * d * 4)
    y, cg, cv = pl.pallas_call(
        functools.partial(_ffn_kernel, n_streams=n_streams, stream_len=stream_len),
        grid=(n_tiles, nj),
        in_specs=[
            row_blk(d), row_blk(d),
            pl.BlockSpec((d, ft), gate_col), pl.BlockSpec((d, ft), val_col),
            pl.BlockSpec((CONV_W, ft), gate_col), pl.BlockSpec((CONV_W, ft), val_col),
            pl.BlockSpec((1, ft), gate_col), pl.BlockSpec((1, ft), val_col),
            hist_blk(0), hist_blk(nj),
            pl.BlockSpec((ft, d), lambda i, j: (j, 0)),
            pl.BlockSpec((1, d), lambda i, j: (0, 0)),
        ],
        out_specs=[row_blk(d), tail_blk, tail_blk],
        out_shape=[jax.ShapeDtypeStruct((rows, d), F32), tail_shape, tail_shape],
        scratch_shapes=[pltpu.VMEM((tm, d), F32),
                        pltpu.VMEM((nj, SUBLANES, ft), F32), pltpu.VMEM((nj, SUBLANES, ft), F32),
                        pltpu.VMEM((2 * (ft // sub) * (sub // LANES), n_streams,
                                    SUBLANES + stream_len, LANES), F32),
                        pltpu.VMEM((d // LANES, tm, LANES), F32)],
        compiler_params=pltpu.CompilerParams(
            dimension_semantics=("arbitrary", "arbitrary"), vmem_limit_bytes=_vmem_limit(est)),
        name="convffn",
    )(hn, x1, w_up, w_up, conv_w, conv_w, conv_b, conv_b, hist, hist, w_down, g_post)
    return y, jnp.concatenate([cg[-1], cv[-1]], axis=-1)


def _layer(x, pos, hist, attend, lw, *, gmlp_len, transposed, emit_vn, n_streams, tm):
    (g_mix_pre, w_in, gv, w_s, b_s, subln_g, w_out, g_mix_post, g_ffn_pre, w_up, conv_w, conv_b,
     w_down, g_ffn_post) = lw
    res = _inproj(x, pos, g_mix_pre, w_in, gv, w_s, b_s, gmlp_len=gmlp_len, transposed=transposed,
                  emit_vn=emit_vn, tm=min(PROJ_ROW_TILE, tm))
    out_a, q, k32, kbf, v32, vbf = res[:6]
    vn = res[6] if emit_vn else None
    out_b = attend(q, kbf, vbf)
    x1, hn2 = _outproj(x, out_a, out_b, w_out, g_mix_post, g_ffn_pre, tm=tm)
    y, new_hist = _ffn(hn2, x1, w_up, conv_w, conv_b, hist, w_down, g_ffn_post,
                       tm=tm, ft=FF_TILE, n_streams=n_streams)
    return y, k32, v32, vn, new_hist


def kernel(x_prompt, x_sample, cache_k, cache_v, state_conv, g_mix_pre, w_in, gmlp_v_gain, gmlp_w_s,
           gmlp_b_s, lambda_q1, lambda_k1, lambda_q2, lambda_k2, subln_g, w_out, g_mix_post,
           g_ffn_pre, w_up, conv_w, conv_b, w_down, g_ffn_post):
    b_p, s_p, d = x_prompt.shape
    b_s, t_new, _ = x_sample.shape
    depth = w_in.shape[0]
    past_len = cache_k.shape[2]
    d_ff = w_down.shape[1]
    assert b_p == 1
    pos_p = (0, s_p, 1)
    pos_s = (past_len, t_new, b_s)
    zero_hist = jnp.zeros((b_p, CONV_W - 1, 2 * d_ff), F32)
    row = lambda a: a.reshape(1, -1)

    yp = x_prompt.reshape(b_p * s_p, d)
    ys = x_sample.reshape(b_s * t_new, d)
    outs = [[] for _ in range(7)]
    for l in range(depth):
        lam_init = 0.8 - 0.6 * math.exp(-0.3 * l)
        lam_vecs = tuple(row(a[l]) for a in (lambda_q1, lambda_k1, lambda_q2, lambda_k2))
        sub_g = row(subln_g[l])
        lw = (row(g_mix_pre[l]), w_in[l].astype(BF16), row(gmlp_v_gain[l]), gmlp_w_s[l], gmlp_b_s[l],
              sub_g, w_out[l].astype(BF16), row(g_mix_post[l]), row(g_ffn_pre[l]),
              w_up[l].astype(BF16), conv_w[l], row(conv_b[l]), w_down[l].astype(BF16),
              row(g_ffn_post[l]))

        tq = min(ATT_Q_TILE, s_p)
        attend_p = lambda q, k, v: _prompt_attention(
            q, k, v, lam_vecs, sub_g, lam_init=lam_init, tq=tq, tk=min(ATT_K_TILE, tq))
        yp, kp, vp, _, cp = _layer(yp, pos_p, zero_hist, attend_p, lw, gmlp_len=GROUP_DIM,
                                   transposed=True, emit_vn=False, n_streams=1,
                                   tm=min(ROW_TILE, s_p))

        ck = cache_k[l].reshape(b_s, past_len, ATT_W)
        cv = cache_v[l]
        attend_s = lambda q, k, v: _sample_attention(
            q, k, v, ck, cv, lam_vecs, sub_g, lam_init=lam_init, t_new=t_new)
        ys, ks, vs, gs, cs = _layer(ys, pos_s, state_conv[l], attend_s, lw, gmlp_len=t_new,
                                    transposed=False, emit_vn=True, n_streams=b_s,
                                    tm=b_s * t_new)

        for acc, val in zip(outs, (
                kp.reshape(b_p, s_p, N_HEADS, 2, HEAD_DIM), vp.reshape(b_p, s_p, N_HEADS, HEAD_W), cp,
                ks.reshape(b_s, t_new, N_HEADS, 2, HEAD_DIM), vs.reshape(b_s, t_new, N_HEADS, HEAD_W),
                gs.reshape(b_s, t_new, N_GROUPS, GROUP_DIM), cs)):
            acc.append(val)

    return (yp.reshape(b_p, s_p, d), ys.reshape(b_s, t_new, d)) + tuple(jnp.stack(o) for o in outs)
```
